```python
import numpy as np
import jax
import jax.numpy as jnp
from jax import lax

D_MODEL = 1024
BATCH = 16
SEQ = 2048
DEPTH = 4

D_MIX = D_MODEL
HEAD_DIM = 64
POOL_WINDOWS = (2, 4, 8, 16)
POOL_GROUPS = 4
POOL_WIDTH = D_MIX // 4
POOL_GDIM = POOL_WIDTH // POOL_GROUPS
NSA_WIDTH = D_MIX // 2
N_HEADS = NSA_WIDTH // HEAD_DIM
KV_HEADS = 2
HEADS_PER_KV = N_HEADS // KV_HEADS
KV_WIDTH = KV_HEADS * HEAD_DIM
CONV_WIDTH = D_MIX - POOL_WIDTH - NSA_WIDTH
CONV_K = 3
CMP_BLOCK = 32
CMP_STRIDE = 16
SEL_BLOCK = 64
SEL_TOPN = 8
WINDOW = 512
Q_BLOCK = 64
D_FF = 2816
FFN_CONV_K = 3
IN_COLS = POOL_WIDTH + NSA_WIDTH + 6 * KV_WIDTH + 3 * N_HEADS + 3 * CONV_WIDTH
EPS = 1e-6
NEG_INF = -1e30
FORCE_SCORE = 1e6

kernel_name = 'hybrid_pool_nsa_shortconv_block'


def _rmsnorm(x, g):
    xf = x.astype(jnp.float32)
    y = xf * lax.rsqrt(jnp.mean(xf * xf, axis=-1, keepdims=True) + EPS)
    return (y * g.astype(jnp.float32)).astype(x.dtype)


def _causal_dwconv(x, w):
    width = w.shape[0]
    return lax.conv_general_dilated(
        x, w[:, None, :].astype(x.dtype), window_strides=(1,),
        padding=[(width - 1, 0)], dimension_numbers=('NWC', 'WIO', 'NWC'),
        feature_group_count=x.shape[-1])


def _alibi_slopes():
    return 2.0 ** (-8.0 * jnp.arange(1, N_HEADS + 1, dtype=jnp.float32) / N_HEADS)


def _pool_mixer(u, pool_w, pool_scale):
    b, t, _ = u.shape
    ug = u.reshape(b, t, POOL_GROUPS, POOL_GDIM).astype(jnp.float32)
    cs = jnp.cumsum(ug, axis=1)
    csp = jnp.concatenate([jnp.zeros_like(cs[:, :1]), cs], axis=1)
    pos = jnp.arange(t)
    means = []
    for gi, w in enumerate(POOL_WINDOWS):
        start = jnp.maximum(pos + 1 - w, 0)
        cnt = (pos + 1 - start).astype(jnp.float32)[None, :, None]
        means.append((cs[:, :, gi] - csp[:, start, gi]) / cnt)
    diff = (jnp.stack(means, axis=2) - ug).astype(u.dtype)
    y = jnp.einsum('btgc,gcd->btgd', diff, pool_w).reshape(b, t, POOL_WIDTH)
    return y * pool_scale


def _compress(kv, pos_emb, w1, w2):
    t = kv.shape[2]
    n_cmp = (t - CMP_BLOCK) // CMP_STRIDE + 1
    idx = np.arange(n_cmp)[:, None] * CMP_STRIDE + np.arange(CMP_BLOCK)[None, :]
    blocks = kv[:, :, idx] + pos_emb
    flat = blocks.reshape(blocks.shape[:3] + (CMP_BLOCK * HEAD_DIM,))
    return jax.nn.gelu(flat @ w1) @ w2


def _cmp_to_sel_matrix(t):
    n_cmp = (t - CMP_BLOCK) // CMP_STRIDE + 1
    n_sel = t // SEL_BLOCK
    c0 = np.arange(n_cmp) * CMP_STRIDE
    s0 = np.arange(n_sel) * SEL_BLOCK
    ov = np.minimum(c0[:, None] + CMP_BLOCK, s0[None, :] + SEL_BLOCK) - np.maximum(c0[:, None], s0[None, :])
    return jnp.asarray(np.clip(ov, 0, None) / CMP_BLOCK, dtype=jnp.float32)


def _nsa_mixer(q, kv, gate_logits, q_g, k_g, cmp_pos, cmp_w1, cmp_w2):
    b, t, _ = q.shape
    G, HG, DH = KV_HEADS, HEADS_PER_KV, HEAD_DIM
    scale = DH ** -0.5
    q = _rmsnorm(q.reshape(b, t, N_HEADS, DH), q_g)
    q = q.reshape(b, t, G, HG, DH).transpose(0, 2, 3, 1, 4)
    k_cmp_raw, v_cmp_raw, k_slc, v_slc, k_win, v_win = [
        a.reshape(b, t, G, DH).transpose(0, 2, 1, 3) for a in jnp.split(kv, 6, axis=-1)]
    slopes = _alibi_slopes().reshape(G, HG)
    pos = jnp.arange(t)

    k_cmp = _rmsnorm(_compress(k_cmp_raw, cmp_pos[0], cmp_w1[0], cmp_w2[0]), k_g[0])
    v_cmp = _compress(v_cmp_raw, cmp_pos[1], cmp_w1[1], cmp_w2[1])
    n_cmp = k_cmp.shape[2]
    cmp_end = jnp.arange(n_cmp) * CMP_STRIDE + CMP_BLOCK - 1
    dist_c = (pos[:, None] - cmp_end[None, :]).astype(jnp.float32)
    valid_c = dist_c >= 0
    s_c = jnp.einsum('bghtd,bgnd->bghtn', q, k_cmp).astype(jnp.float32) * scale \
        - slopes[:, :, None, None] * dist_c
    p_cmp = jax.nn.softmax(jnp.where(valid_c, s_c, NEG_INF), axis=-1) \
        * jnp.any(valid_c, axis=-1)[:, None].astype(jnp.float32)
    o_cmp = jnp.einsum('bghtn,bgnd->bghtd', p_cmp.astype(v_cmp.dtype), v_cmp)

    n_sel_blocks = t // SEL_BLOCK
    n_top = min(SEL_TOPN, n_sel_blocks)
    imp = jnp.einsum('bghtn,nj->bgtj', p_cmp, _cmp_to_sel_matrix(t))
    blk = jnp.arange(n_sel_blocks)[None, :]
    cur = (pos // SEL_BLOCK)[:, None]
    forced = (blk == 0) | (blk == cur) | (blk == cur - 1)
    imp = jnp.where(forced, FORCE_SCORE, jnp.where(blk <= cur, imp, -FORCE_SCORE))
    _, sel_idx = lax.top_k(imp, n_top)

    k_slc = _rmsnorm(k_slc, k_g[1]).reshape(b, G, n_sel_blocks, SEL_BLOCK, DH)
    v_slc = v_slc.reshape(b, G, n_sel_blocks, SEL_BLOCK, DH)
    pad = ((0, 0), (0, 0), (WINDOW, 0), (0, 0))
    k_win = jnp.pad(_rmsnorm(k_win, k_g[2]), pad)
    v_win = jnp.pad(v_win, pad)
    bi = jnp.arange(b)[:, None, None, None]
    gi = jnp.arange(G)[None, :, None, None]
    in_blk = jnp.arange(SEL_BLOCK)
    win_off = jnp.arange(WINDOW + Q_BLOCK)

    def q_block(c):
        t0 = c * Q_BLOCK
        qc = lax.dynamic_slice_in_dim(q, t0, Q_BLOCK, axis=3)
        tq = t0 + jnp.arange(Q_BLOCK)
        idx = lax.dynamic_slice_in_dim(sel_idx, t0, Q_BLOCK, axis=2)
        ks = k_slc[bi, gi, idx]
        vs = v_slc[bi, gi, idx]
        kpos = idx[..., None] * SEL_BLOCK + in_blk
        d_s = (tq[None, None, :, None, None] - kpos)[:, :, None].astype(jnp.float32)
        s_s = jnp.einsum('bghqd,bgqnld->bghqnl', qc, ks).astype(jnp.float32) * scale \
            - slopes[None, :, :, None, None, None] * d_s
        s_s = jnp.where(d_s >= 0, s_s, NEG_INF).reshape(b, G, HG, Q_BLOCK, n_top * SEL_BLOCK)
        p_s = jax.nn.softmax(s_s, axis=-1).reshape(b, G, HG, Q_BLOCK, n_top, SEL_BLOCK)
        o_s = jnp.einsum('bghqnl,bgqnld->bghqd', p_s.astype(vs.dtype), vs)
        kw = lax.dynamic_slice_in_dim(k_win, t0, WINDOW + Q_BLOCK, axis=2)
        vw = lax.dynamic_slice_in_dim(v_win, t0, WINDOW + Q_BLOCK, axis=2)
        kp = t0 - WINDOW + win_off
        d_w = tq[:, None] - kp[None, :]
        ok = (d_w >= 0) & (d_w < WINDOW) & (kp[None, :] >= 0)
        s_w = jnp.einsum('bghqd,bgkd->bghqk', qc, kw).astype(jnp.float32) * scale \
            - slopes[:, :, None, None] * d_w.astype(jnp.float32)
        p_w = jax.nn.softmax(jnp.where(ok, s_w, NEG_INF), axis=-1)
        o_w = jnp.einsum('bghqk,bgkd->bghqd', p_w.astype(vw.dtype), vw)
        return o_s, o_w

    o_slc, o_win = lax.map(q_block, jnp.arange(t // Q_BLOCK))

    def unblock(o):
        return o.transpose(1, 2, 3, 0, 4, 5).reshape(b, G, HG, t, DH)

    gates = jax.nn.sigmoid(gate_logits.astype(jnp.float32)).reshape(b, t, 3, G, HG)
    gates = gates.transpose(2, 0, 3, 4, 1)[..., None].astype(q.dtype)
    o = gates[0] * o_cmp + gates[1] * unblock(o_slc) + gates[2] * unblock(o_win)
    return o.transpose(0, 3, 1, 2, 4).reshape(b, t, NSA_WIDTH)


def setup_inputs(seed: int = 0) -> dict:
    key = jax.random.key(seed)
    ks = jax.random.split(key, 16)

    def nrm(k, shape, s):
        return jax.random.normal(k, shape, jnp.float32) * s

    return {
        'x': nrm(ks[0], (BATCH, SEQ, D_MODEL), 1.0),
        'norm1_g': 1.0 + nrm(ks[1], (DEPTH, D_MODEL), 0.02),
        'w_in': nrm(ks[2], (DEPTH, D_MODEL, IN_COLS), D_MODEL ** -0.5),
        'pool_w': nrm(ks[3], (DEPTH, POOL_GROUPS, POOL_GDIM, POOL_GDIM), POOL_GDIM ** -0.5),
        'pool_scale': 1.0 + nrm(ks[4], (DEPTH, POOL_WIDTH), 0.02),
        'q_norm_g': 1.0 + nrm(ks[5], (DEPTH, HEAD_DIM), 0.02),
        'k_norm_g': 1.0 + nrm(ks[6], (DEPTH, 3, HEAD_DIM), 0.02),
        'cmp_pos': nrm(ks[7], (DEPTH, 2, CMP_BLOCK, HEAD_DIM), 0.1),
        'cmp_w1': nrm(ks[8], (DEPTH, 2, CMP_BLOCK * HEAD_DIM, HEAD_DIM), (CMP_BLOCK * HEAD_DIM) ** -0.5),
        'cmp_w2': nrm(ks[9], (DEPTH, 2, HEAD_DIM, HEAD_DIM), HEAD_DIM ** -0.5),
        'sconv_w': nrm(ks[10], (DEPTH, CONV_K, CONV_WIDTH), CONV_K ** -0.5),
        'w_out': nrm(ks[11], (DEPTH, D_MIX, D_MODEL), D_MIX ** -0.5),
        'norm2_g': 1.0 + nrm(ks[12], (DEPTH, D_MODEL), 0.02),
        'ffn_up': nrm(ks[13], (DEPTH, D_MODEL, 2 * D_FF), D_MODEL ** -0.5),
        'ffn_conv': nrm(ks[14], (DEPTH, FFN_CONV_K, D_FF), FFN_CONV_K ** -0.5),
        'ffn_down': nrm(ks[15], (DEPTH, D_FF, D_MODEL), D_FF ** -0.5),
    }


def reference(x, norm1_g, w_in, pool_w, pool_scale, q_norm_g, k_norm_g, cmp_pos, cmp_w1, cmp_w2,
              sconv_w, w_out, norm2_g, ffn_up, ffn_conv, ffn_down):
    sizes = [POOL_WIDTH, NSA_WIDTH, 6 * KV_WIDTH, 3 * N_HEADS, CONV_WIDTH, CONV_WIDTH]
    split_at = [int(v) for v in np.cumsum(sizes)]
    for l in range(DEPTH):
        h = _rmsnorm(x, norm1_g[l])
        u_pool, u_q, u_kv, u_gate, c_b, c_c, c_x = jnp.split(h @ w_in[l], split_at, axis=-1)
        y_pool = _pool_mixer(u_pool, pool_w[l], pool_scale[l])
        y_nsa = _nsa_mixer(u_q, u_kv, u_gate, q_norm_g[l], k_norm_g[l], cmp_pos[l], cmp_w1[l], cmp_w2[l])
        y_conv = c_b * _causal_dwconv(c_c * c_x, sconv_w[l])
        x = x + jnp.concatenate([y_pool, y_nsa, y_conv], axis=-1) @ w_out[l]
        h = _rmsnorm(x, norm2_g[l])
        a, g = jnp.split(h @ ffn_up[l], 2, axis=-1)
        x = x + (jax.nn.silu(_causal_dwconv(a, ffn_conv[l])) * g) @ ffn_down[l]
    return x
```

```python
import functools

import numpy as np
import jax
import jax.numpy as jnp
from jax import lax
from jax.experimental import pallas as pl
from jax.experimental.pallas import tpu as pltpu

D_MODEL = 1024
SEQ = 2048
HEAD_DIM = 64
POOL_WINDOWS = (2, 4, 8, 16)
POOL_GROUPS = 4
POOL_WIDTH = 256
POOL_GDIM = POOL_WIDTH // POOL_GROUPS
NSA_WIDTH = 512
N_HEADS = 8
KV_HEADS = 2
HEADS_PER_KV = N_HEADS // KV_HEADS
KV_WIDTH = KV_HEADS * HEAD_DIM
CONV_WIDTH = 256
CMP_BLOCK = 32
CMP_STRIDE = 16
N_CMP = (SEQ - CMP_BLOCK) // CMP_STRIDE + 1
N_CHUNK = SEQ // CMP_STRIDE
SEL_BLOCK = 64
N_SEL = SEQ // SEL_BLOCK
SEL_TOPN = 8
WINDOW = 512
D_FF = 2816
EPS = 1e-6
NEG_INF = -1e30
FORCE_SCORE = 1e6
QK_SCALE = HEAD_DIM ** -0.5

LANES = 128
GATE_PAD = LANES
KVG_WIDTH = 4 * HEAD_DIM
HALO = 16

TM_PROJ = 512
TQ = 256
FF_CHUNK = 256
VMEM_LIMIT = 56 * 1024 * 1024

F32 = jnp.float32
BF16 = jnp.bfloat16


def _rms(x, g):
    return x * lax.rsqrt(jnp.mean(x * x, axis=-1, keepdims=True) + EPS) * g


def _dot(a, b):
    return jnp.dot(a, b, preferred_element_type=F32)


def _dot_nt(a, b):
    return lax.dot_general(a, b, (((1,), (1,)), ((), ())), preferred_element_type=F32)


IN_SEGS = (("pool", POOL_WIDTH), ("q", NSA_WIDTH), ("kvg", KV_HEADS * KVG_WIDTH),
           ("kvc", 2 * KV_WIDTH), ("conv", 3 * CONV_WIDTH), ("gate", GATE_PAD))


def _in_proj_kernel(x_ref, g_ref, w_ref, *out_refs):
    h = _rms(x_ref[0], g_ref[...]).astype(BF16)
    off = 0
    for (_, width), o_ref in zip(IN_SEGS, out_refs):
        o_ref[0] = _dot(h, w_ref[:, off:off + width])
        off += width


def _in_proj(x, g, w):
    b, t, d = x.shape
    n = w.shape[1]
    tm = TM_PROJ
    out_shape = [jax.ShapeDtypeStruct((b, t, width), F32) for _, width in IN_SEGS]
    out_specs = [pl.BlockSpec((1, tm, width), lambda i, j: (i, j, 0)) for _, width in IN_SEGS]
    return pl.pallas_call(
        _in_proj_kernel,
        grid=(b, t // tm),
        in_specs=[pl.BlockSpec((1, tm, d), lambda i, j: (i, j, 0)),
                  pl.BlockSpec((1, d), lambda i, j: (0, 0)),
                  pl.BlockSpec((d, n), lambda i, j: (0, 0))],
        out_specs=out_specs,
        out_shape=out_shape,
        compiler_params=pltpu.CompilerParams(
            dimension_semantics=("arbitrary", "arbitrary"), vmem_limit_bytes=VMEM_LIMIT),
        name="in_proj",
    )(x, g, w)


def _compress_kernel(kc_ref, vc_ref, plo_ref, phi_ref, w1lo_ref, w1hi_ref, w2_ref, kg_ref,
                     ko_ref, vo_ref):
    for i, (src, dst) in enumerate(((kc_ref, ko_ref), (vc_ref, vo_ref))):
        c = src[0, 0]
        lo = _dot((c + plo_ref[i]).astype(BF16), w1lo_ref[i])
        hi = _dot((c + phi_ref[i]).astype(BF16), w1hi_ref[i])
        hid = jax.nn.gelu(lo + pltpu.roll(hi, N_CHUNK - 1, 0))
        o = _dot(hid.astype(BF16), w2_ref[i])
        if i == 0:
            o = _rms(o, kg_ref[...])
        dst[0, 0] = o.astype(BF16)


def _compress(kc, vc, plo, phi, w1lo, w1hi, w2, kg):
    b, g = kc.shape[:2]
    blk = pl.BlockSpec((1, 1, N_CHUNK, CMP_STRIDE * HEAD_DIM), lambda i, j: (i, j, 0, 0))
    full = lambda a: pl.BlockSpec(a.shape, lambda i, j: (0,) * a.ndim)
    oblk = pl.BlockSpec((1, 1, N_CHUNK, HEAD_DIM), lambda i, j: (i, j, 0, 0))
    osd = jax.ShapeDtypeStruct((b, g, N_CHUNK, HEAD_DIM), BF16)
    return pl.pallas_call(
        _compress_kernel,
        grid=(b, g),
        in_specs=[blk, blk, full(plo), full(phi), full(w1lo), full(w1hi), full(w2), full(kg)],
        out_specs=[oblk, oblk],
        out_shape=[osd, osd],
        compiler_params=pltpu.CompilerParams(
            dimension_semantics=("arbitrary", "arbitrary"), vmem_limit_bytes=VMEM_LIMIT),
        name="compress",
    )(kc, vc, plo, phi, w1lo, w1hi, w2, kg)


def _flash(q_s, k_s, v_s, slopes, t0, qi, lo, mask_fn):
    tq = TQ
    row = lax.broadcasted_iota(jnp.int32, (tq, tq), 0)
    col = lax.broadcasted_iota(jnp.int32, (tq, tq), 1)
    krel = lax.broadcasted_iota(jnp.int32, (1, tq), 1)

    def step(jt, carry, diag):
        k0 = pl.multiple_of(jt * tq, tq)
        kt = k_s[pl.ds(k0, tq), :]
        vt = v_s[pl.ds(k0, tq), :]
        mask = mask_fn(jt, diag)
        if diag:
            mask = jnp.where(row >= col, mask, NEG_INF)
        kpos = (k0 - t0 + krel).astype(F32)
        out = []
        for h in range(HEADS_PER_KV):
            m, l, acc = carry[h]
            s = _dot_nt(q_s[h * tq:(h + 1) * tq, :], kt) + (mask + slopes[h] * kpos)
            m_new = jnp.maximum(m, jnp.max(s, axis=-1, keepdims=True))
            alpha = jnp.exp(m - m_new)
            p = jnp.exp(s - m_new)
            l = alpha * l + jnp.sum(p, axis=-1, keepdims=True)
            acc = alpha * acc + _dot(p.astype(BF16), vt)
            out.append((m_new, l, acc))
        return tuple(out)

    init = tuple((jnp.full((tq, 1), NEG_INF, F32), jnp.zeros((tq, 1), F32),
                  jnp.zeros((tq, HEAD_DIM), F32)) for _ in range(HEADS_PER_KV))
    carry = lax.fori_loop(lo, qi, lambda jt, c: step(jt, c, False), init)
    carry = step(qi, carry, True)
    return [acc / l for (_, l, acc) in carry]


def _nsa_kernel(q_ref, kv_ref, kc_ref, vc_ref, gate_ref, qg_ref, kg_ref, selt_ref, eneg_ref,
                o_ref, ks_s, vs_s, kw_s, vw_s, q_s):
    tq = TQ
    g = pl.program_id(1)
    qi = pl.program_id(2)
    t0 = qi * tq

    @pl.when(qi == 0)
    def _prep_keys():
        ks_s[...] = _rms(kv_ref[0, :, 0:HEAD_DIM], kg_ref[0:1, :]).astype(BF16)
        vs_s[...] = kv_ref[0, :, HEAD_DIM:2 * HEAD_DIM].astype(BF16)
        kw_s[...] = _rms(kv_ref[0, :, 2 * HEAD_DIM:3 * HEAD_DIM], kg_ref[1:2, :]).astype(BF16)
        vw_s[...] = kv_ref[0, :, 3 * HEAD_DIM:4 * HEAD_DIM].astype(BF16)

    for h in range(HEADS_PER_KV):
        qh = _rms(q_ref[0, :, h * HEAD_DIM:(h + 1) * HEAD_DIM], qg_ref[...])
        q_s[h * tq:(h + 1) * tq, :] = (qh * QK_SCALE).astype(BF16)

    slopes = [jnp.where(g == 0, 2.0 ** -(h + 1), 2.0 ** -(h + 1 + HEADS_PER_KV)).astype(F32)
              for h in range(HEADS_PER_KV)]

    t_col = t0 + lax.broadcasted_iota(jnp.int32, (tq, 1), 0)
    n_row = lax.broadcasted_iota(jnp.int32, (1, N_CHUNK), 1)
    dist = (t_col - (n_row * CMP_STRIDE + (CMP_BLOCK - 1))).astype(F32)
    valid = dist >= 0
    any_valid = (t_col >= CMP_BLOCK - 1).astype(F32)
    kc = kc_ref[0, 0]
    vc = vc_ref[0, 0]
    o_cmp = []
    imp_t = jnp.zeros((N_SEL, tq), F32)
    for h in range(HEADS_PER_KV):
        s = _dot_nt(q_s[h * tq:(h + 1) * tq, :], kc) - slopes[h] * dist
        s = jnp.where(valid, s, NEG_INF)
        e = jnp.exp(s - jnp.max(s, axis=-1, keepdims=True))
        p = (e / jnp.sum(e, axis=-1, keepdims=True) * any_valid).astype(BF16)
        o_cmp.append(_dot(p, vc))
        imp_t = imp_t + _dot_nt(selt_ref[...], p)

    blk = lax.broadcasted_iota(jnp.int32, (N_SEL, tq), 0)
    cur = (t0 + lax.broadcasted_iota(jnp.int32, (N_SEL, tq), 1)) // SEL_BLOCK
    forced = (blk == 0) | (blk == cur) | (blk == cur - 1)
    val = jnp.where(forced, FORCE_SCORE, jnp.where(blk <= cur, imp_t, -FORCE_SCORE))
    rank = jnp.zeros((N_SEL, tq), F32)
    for i in range(N_SEL):
        vi = val[i:i + 1, :]
        ahead = jnp.where(blk > i, jnp.where(vi >= val, 1.0, 0.0), jnp.where(vi > val, 1.0, 0.0))
        rank = rank + ahead
    not_sel = jnp.where(rank < SEL_TOPN, 0.0, 1.0).T.astype(BF16)

    def slc_mask(jt, diag):
        return _dot(not_sel, eneg_ref[jt])

    def win_mask(jt, diag):
        if diag:
            return jnp.zeros((tq, tq), F32)
        row = lax.broadcasted_iota(jnp.int32, (tq, tq), 0)
        col = lax.broadcasted_iota(jnp.int32, (tq, tq), 1)
        d = (t0 - jt * tq) + row - col
        return jnp.where(d < WINDOW, 0.0, NEG_INF)

    o_slc = _flash(q_s, ks_s, vs_s, slopes, t0, qi, 0, slc_mask)
    o_win = _flash(q_s, kw_s, vw_s, slopes, t0, qi, jnp.maximum(qi - WINDOW // tq, 0), win_mask)

    gates = jax.nn.sigmoid(gate_ref[0])
    for h in range(HEADS_PER_KV):
        cols = [br * N_HEADS + g * HEADS_PER_KV + h for br in range(3)]
        lane = lax.broadcasted_iota(jnp.int32, (1, GATE_PAD), 1)
        gsel = [jnp.sum(jnp.where(lane == c, gates, 0.0), axis=-1, keepdims=True) for c in cols]
        o = gsel[0] * o_cmp[h] + gsel[1] * o_slc[h] + gsel[2] * o_win[h]
        o_ref[0, :, h * HEAD_DIM:(h + 1) * HEAD_DIM] = o.astype(BF16)


def _nsa(q, kvg, kcmp, vcmp, gate, qg, kg, selt, eneg):
    b, t, _ = q.shape
    tq = TQ
    gq = HEADS_PER_KV * HEAD_DIM
    full = lambda a: pl.BlockSpec(a.shape, lambda i, j, k: (0,) * a.ndim)
    cmp_spec = pl.BlockSpec((1, 1, N_CHUNK, HEAD_DIM), lambda i, j, k: (i, j, 0, 0))
    return pl.pallas_call(
        _nsa_kernel,
        grid=(b, KV_HEADS, t // tq),
        in_specs=[pl.BlockSpec((1, tq, gq), lambda i, j, k: (i, k, j)),
                  pl.BlockSpec((1, t, KVG_WIDTH), lambda i, j, k: (i, 0, j)),
                  cmp_spec, cmp_spec,
                  pl.BlockSpec((1, tq, GATE_PAD), lambda i, j, k: (i, k, 0)),
                  full(qg), full(kg), full(selt), full(eneg)],
        out_specs=pl.BlockSpec((1, tq, gq), lambda i, j, k: (i, k, j)),
        out_shape=jax.ShapeDtypeStruct((b, t, NSA_WIDTH), BF16),
        scratch_shapes=[pltpu.VMEM((t, HEAD_DIM), BF16)] * 4
        + [pltpu.VMEM((HEADS_PER_KV * tq, HEAD_DIM), BF16)],
        compiler_params=pltpu.CompilerParams(
            dimension_semantics=("arbitrary", "arbitrary", "arbitrary"),
            vmem_limit_bytes=VMEM_LIMIT),
        name="nsa",
    )(q, kvg, kcmp, vcmp, gate, qg, kg, selt, eneg)


def _out_proj_kernel(pool_ref, conv_ref, nsa_ref, x_ref, pbd_ref, pscale_ref, sw_ref, wout_ref,
                     g2_ref, x1_ref, h2_ref, cat_s, ptail_s, ctail_s):
    tm = TM_PROJ
    ti = pl.program_id(1)

    @pl.when(ti == 0)
    def _reset():
        ptail_s[...] = jnp.zeros_like(ptail_s)
        ctail_s[...] = jnp.zeros_like(ctail_s)

    u = pool_ref[0]
    ext = jnp.concatenate([ptail_s[...], u], axis=0)
    sums = []
    acc = ext
    for shift in (1, 2, 4, 8):
        acc = acc + pltpu.roll(acc, shift, 0)
        sums.append(acc[HALO:])
    grp = lax.broadcasted_iota(jnp.int32, (1, POOL_WIDTH), 1) // POOL_GDIM
    wsum = jnp.where(grp == 0, sums[0], jnp.where(grp == 1, sums[1],
                                                  jnp.where(grp == 2, sums[2], sums[3])))
    width = jnp.where(grp == 0, POOL_WINDOWS[0], jnp.where(
        grp == 1, POOL_WINDOWS[1], jnp.where(grp == 2, POOL_WINDOWS[2], POOL_WINDOWS[3])))
    tpos = ti * tm + lax.broadcasted_iota(jnp.int32, (tm, 1), 0) + 1
    cnt = jnp.minimum(tpos, width).astype(F32)
    diff = (wsum / cnt - u).astype(BF16)
    cat_s[:, 0:POOL_WIDTH] = (_dot(diff, pbd_ref[...]) * pscale_ref[...]).astype(BF16)
    ptail_s[...] = u[tm - HALO:]

    cat_s[:, POOL_WIDTH:POOL_WIDTH + NSA_WIDTH] = nsa_ref[0]

    cb = conv_ref[0, :, 0:CONV_WIDTH]
    prod = conv_ref[0, :, CONV_WIDTH:2 * CONV_WIDTH] * conv_ref[0, :, 2 * CONV_WIDTH:3 * CONV_WIDTH]
    pext = jnp.concatenate([ctail_s[...], prod], axis=0)
    conv = (sw_ref[2:3, :] * pext + sw_ref[1:2, :] * pltpu.roll(pext, 1, 0)
            + sw_ref[0:1, :] * pltpu.roll(pext, 2, 0))[HALO:]
    cat_s[:, POOL_WIDTH + NSA_WIDTH:] = (cb * conv).astype(BF16)
    ctail_s[...] = prod[tm - HALO:]

    x1 = x_ref[0] + _dot(cat_s[...], wout_ref[...])
    x1_ref[0] = x1
    h2_ref[0] = _rms(x1, g2_ref[...]).astype(BF16)


def _out_proj(pool, conv, nsa, x, pbd, pscale, sw, wout, g2):
    b, t, d = x.shape
    tm = TM_PROJ
    tile = lambda width: pl.BlockSpec((1, tm, width), lambda i, j: (i, j, 0))
    full = lambda a: pl.BlockSpec(a.shape, lambda i, j: (0,) * a.ndim)
    return pl.pallas_call(
        _out_proj_kernel,
        grid=(b, t // tm),
        in_specs=[tile(POOL_WIDTH), tile(3 * CONV_WIDTH), tile(NSA_WIDTH), tile(d),
                  full(pbd), full(pscale), full(sw), full(wout), full(g2)],
        out_specs=[tile(d), tile(d)],
        out_shape=[jax.ShapeDtypeStruct((b, t, d), F32), jax.ShapeDtypeStruct((b, t, d), BF16)],
        scratch_shapes=[pltpu.VMEM((tm, d), BF16),
                        pltpu.VMEM((HALO, POOL_WIDTH), F32),
                        pltpu.VMEM((HALO, CONV_WIDTH), F32)],
        compiler_params=pltpu.CompilerParams(
            dimension_semantics=("arbitrary", "arbitrary"), vmem_limit_bytes=VMEM_LIMIT),
        name="out_proj",
    )(pool, conv, nsa, x, pbd, pscale, sw, wout, g2)


def _ffn_kernel(h2_ref, x1_ref, wup_ref, cw_ref, wd_ref, o_ref, z_s, atail_s):
    tm = TM_PROJ
    ti = pl.program_id(1)

    @pl.when(ti == 0)
    def _reset():
        atail_s[...] = jnp.zeros_like(atail_s)

    h2 = h2_ref[0]
    for c in range(D_FF // FF_CHUNK):
        lo, hi = c * FF_CHUNK, (c + 1) * FF_CHUNK
        a = _dot(h2, wup_ref[:, lo:hi])
        gate = _dot(h2, wup_ref[:, D_FF + lo:D_FF + hi])
        aext = jnp.concatenate([atail_s[:, lo:hi], a], axis=0)
        conv = (cw_ref[2:3, lo:hi] * aext + cw_ref[1:2, lo:hi] * pltpu.roll(aext, 1, 0)
                + cw_ref[0:1, lo:hi] * pltpu.roll(aext, 2, 0))[HALO:]
        z_s[:, lo:hi] = (jax.nn.silu(conv) * gate).astype(BF16)
        atail_s[:, lo:hi] = a[tm - HALO:]
    o_ref[0] = x1_ref[0] + _dot(z_s[...], wd_ref[...])


def _ffn(h2, x1, wup, cw, wd):
    b, t, d = x1.shape
    tm = TM_PROJ
    tile = pl.BlockSpec((1, tm, d), lambda i, j: (i, j, 0))
    full = lambda a: pl.BlockSpec(a.shape, lambda i, j: (0,) * a.ndim)
    return pl.pallas_call(
        _ffn_kernel,
        grid=(b, t // tm),
        in_specs=[tile, tile, full(wup), full(cw), full(wd)],
        out_specs=tile,
        out_shape=jax.ShapeDtypeStruct((b, t, d), F32),
        scratch_shapes=[pltpu.VMEM((tm, D_FF), BF16), pltpu.VMEM((HALO, D_FF), F32)],
        compiler_params=pltpu.CompilerParams(
            dimension_semantics=("arbitrary", "arbitrary"), vmem_limit_bytes=VMEM_LIMIT),
        name="ffn",
    )(h2, x1, wup, cw, wd)


def _selection_constants():
    c0 = np.arange(N_CMP) * CMP_STRIDE
    s0 = np.arange(N_SEL) * SEL_BLOCK
    ov = (np.minimum(c0[:, None] + CMP_BLOCK, s0[None, :] + SEL_BLOCK)
          - np.maximum(c0[:, None], s0[None, :]))
    frac = np.zeros((N_CHUNK, N_SEL), np.float32)
    frac[:N_CMP] = np.clip(ov, 0, None) / CMP_BLOCK
    kblk = (np.arange(SEQ) // SEL_BLOCK).reshape(SEQ // TQ, 1, TQ)
    eneg = np.where(kblk == np.arange(N_SEL).reshape(1, N_SEL, 1), NEG_INF, 0.0).astype(np.float32)
    return jnp.asarray(frac.T, BF16), jnp.asarray(eneg, BF16)


def _layer_weights(w_in, pool_w):
    o_q = POOL_WIDTH
    o_kv = o_q + NSA_WIDTH
    o_gate = o_kv + 6 * KV_WIDTH
    o_conv = o_gate + 3 * N_HEADS
    kv = [w_in[:, o_kv + i * KV_WIDTH:o_kv + (i + 1) * KV_WIDTH] for i in range(6)]
    grp = lambda a, g: a[:, g * HEAD_DIM:(g + 1) * HEAD_DIM]
    kvg = [grp(kv[i], g) for g in range(KV_HEADS) for i in (2, 3, 4, 5)]
    gate = jnp.pad(w_in[:, o_gate:o_conv], ((0, 0), (0, GATE_PAD - 3 * N_HEADS)))
    w = jnp.concatenate([w_in[:, :o_kv]] + kvg + [kv[0], kv[1], w_in[:, o_conv:], gate], axis=1)
    pbd = jax.scipy.linalg.block_diag(*[pool_w[i] for i in range(POOL_GROUPS)])
    return w.astype(BF16), pbd.astype(BF16)


def kernel(x, norm1_g, w_in, pool_w, pool_scale, q_norm_g, k_norm_g, cmp_pos, cmp_w1, cmp_w2,
           sconv_w, w_out, norm2_g, ffn_up, ffn_conv, ffn_down):
    depth = w_in.shape[0]
    b, t, _ = x.shape
    selt, eneg = _selection_constants()
    half = CMP_STRIDE * HEAD_DIM
    for l in range(depth):
        w, pbd = _layer_weights(w_in[l], pool_w[l])
        pool, q, kvg, kvc, conv, gate = _in_proj(x, norm1_g[l][None, :], w)
        kvc = kvc.reshape(b, t, 2, KV_HEADS, HEAD_DIM).transpose(2, 0, 3, 1, 4)
        kvc = kvc.reshape(2, b, KV_HEADS, N_CHUNK, half)
        pos = cmp_pos[l].reshape(2, 2, 1, half)
        w1 = cmp_w1[l].reshape(2, 2, half, HEAD_DIM).astype(BF16)
        kcmp, vcmp = _compress(kvc[0], kvc[1], pos[:, 0], pos[:, 1], w1[:, 0], w1[:, 1],
                               cmp_w2[l].astype(BF16), k_norm_g[l][0:1])
        nsa = _nsa(q, kvg, kcmp, vcmp, gate, q_norm_g[l][None, :], k_norm_g[l][1:3], selt, eneg)
        x1, h2 = _out_proj(pool, conv, nsa, x, pbd, pool_scale[l][None, :], sconv_w[l],
                           w_out[l].astype(BF16), norm2_g[l][None, :])
        x = _ffn(h2, x1, ffn_up[l].astype(BF16), ffn_conv[l], ffn_down[l].astype(BF16))
    return x
```

```python
import numpy as np
import jax
import jax.numpy as jnp
from jax import lax
from jax.experimental import pallas as pl
from jax.experimental.pallas import tpu as pltpu

D_MODEL = 1024
SEQ = 2048
HEAD_DIM = 64
POOL_WINDOWS = (2, 4, 8, 16)
POOL_GROUPS = 4
POOL_WIDTH = 256
POOL_GDIM = POOL_WIDTH // POOL_GROUPS
NSA_WIDTH = 512
N_HEADS = 8
KV_HEADS = 2
HEADS_PER_KV = N_HEADS // KV_HEADS
KV_WIDTH = KV_HEADS * HEAD_DIM
CONV_WIDTH = 256
CMP_BLOCK = 32
CMP_STRIDE = 16
N_CMP = (SEQ - CMP_BLOCK) // CMP_STRIDE + 1
N_CHUNK = SEQ // CMP_STRIDE
SEL_BLOCK = 64
N_SEL = SEQ // SEL_BLOCK
SEL_TOPN = 8
WINDOW = 512
D_FF = 2816
EPS = 1e-6
NEG_INF = -1e30
FORCE_SCORE = 1e6
QK_SCALE = HEAD_DIM ** -0.5

LANES = 128
GATE_PAD = LANES
KVG_WIDTH = 4 * HEAD_DIM
HALO = 16
AUG = 2 * HEAD_DIM
POS_SPLIT = 256
ROW_POS = HEAD_DIM
ROW_SEL = AUG - N_SEL

TM_PROJ = 512
TQ = 256
FF_CHUNK = 256
VMEM_LIMIT = 56 * 1024 * 1024

assert WINDOW % TQ == 0 and SEQ % TQ == 0 and TQ % SEL_BLOCK == 0

F32 = jnp.float32
BF16 = jnp.bfloat16


def _rms(x, g):
    return x * lax.rsqrt(jnp.mean(x * x, axis=-1, keepdims=True) + EPS) * g


def _dot(a, b):
    return jnp.dot(a, b, preferred_element_type=F32)


IN_SEGS = (("pool", POOL_WIDTH), ("q", NSA_WIDTH), ("kvg", KV_HEADS * KVG_WIDTH),
           ("kvc", 2 * KV_WIDTH), ("conv", 3 * CONV_WIDTH), ("gate", GATE_PAD))


def _in_proj_kernel(x_ref, g_ref, w_ref, *out_refs):
    h = _rms(x_ref[0], g_ref[...]).astype(BF16)
    off = 0
    for (_, width), o_ref in zip(IN_SEGS, out_refs):
        o_ref[0] = _dot(h, w_ref[:, off:off + width])
        off += width


def _in_proj(x, g, w):
    b, t, d = x.shape
    n = w.shape[1]
    tm = TM_PROJ
    out_shape = [jax.ShapeDtypeStruct((b, t, width), F32) for _, width in IN_SEGS]
    out_specs = [pl.BlockSpec((1, tm, width), lambda i, j: (i, j, 0)) for _, width in IN_SEGS]
    return pl.pallas_call(
        _in_proj_kernel,
        grid=(b, t // tm),
        in_specs=[pl.BlockSpec((1, tm, d), lambda i, j: (i, j, 0)),
                  pl.BlockSpec((1, d), lambda i, j: (0, 0)),
                  pl.BlockSpec((d, n), lambda i, j: (0, 0))],
        out_specs=out_specs,
        out_shape=out_shape,
        compiler_params=pltpu.CompilerParams(
            dimension_semantics=("arbitrary", "arbitrary"), vmem_limit_bytes=VMEM_LIMIT),
        name="in_proj",
    )(x, g, w)


def _compress_kernel(kc_ref, vc_ref, plo_ref, phi_ref, w1lo_ref, w1hi_ref, w2_ref, kg_ref, cx_ref,
                     ko_ref, vo_ref):
    outs = []
    for i, src in enumerate((kc_ref, vc_ref)):
        c = src[0, 0]
        lo = _dot((c + plo_ref[i]).astype(BF16), w1lo_ref[i])
        hi = _dot((c + phi_ref[i]).astype(BF16), w1hi_ref[i])
        hid = jax.nn.gelu(lo + pltpu.roll(hi, N_CHUNK - 1, 0))
        outs.append(_dot(hid.astype(BF16), w2_ref[i]))
    ko_ref[0, 0] = jnp.concatenate([_rms(outs[0], kg_ref[...]).astype(BF16), cx_ref[...]], axis=1)
    v_pad = jnp.concatenate([outs[1], jnp.zeros_like(outs[1])], axis=1)
    vo_ref[0, 0] = v_pad.T[0:HEAD_DIM].astype(BF16)


def _compress(kc, vc, plo, phi, w1lo, w1hi, w2, kg, cx):
    b, g = kc.shape[:2]
    blk = pl.BlockSpec((1, 1, N_CHUNK, CMP_STRIDE * HEAD_DIM), lambda i, j: (i, j, 0, 0))
    full = lambda a: pl.BlockSpec(a.shape, lambda i, j: (0,) * a.ndim)
    return pl.pallas_call(
        _compress_kernel,
        grid=(b, g),
        in_specs=[blk, blk, full(plo), full(phi), full(w1lo), full(w1hi), full(w2), full(kg),
                  full(cx)],
        out_specs=[pl.BlockSpec((1, 1, N_CHUNK, AUG), lambda i, j: (i, j, 0, 0)),
                   pl.BlockSpec((1, 1, HEAD_DIM, N_CHUNK), lambda i, j: (i, j, 0, 0))],
        out_shape=[jax.ShapeDtypeStruct((b, g, N_CHUNK, AUG), BF16),
                   jax.ShapeDtypeStruct((b, g, HEAD_DIM, N_CHUNK), BF16)],
        compiler_params=pltpu.CompilerParams(
            dimension_semantics=("arbitrary", "arbitrary"), vmem_limit_bytes=VMEM_LIMIT),
        name="compress",
    )(kc, vc, plo, phi, w1lo, w1hi, w2, kg, cx)


def _attn_step(ka_s, vt_s, qt_s, m_s, l_s, acc_s, jt, mode):
    tq = TQ
    k0 = pl.multiple_of(jt * tq, tq)
    ka = ka_s[pl.ds(k0, tq), :]
    vt = vt_s[jt]
    if mode != "full":
        key = lax.broadcasted_iota(jnp.int32, (tq, tq), 0)
        qry = lax.broadcasted_iota(jnp.int32, (tq, tq), 1)
        keep = key <= qry if mode == "causal" else key > qry
    scores = [_dot(ka, qt_s[h]) for h in range(HEADS_PER_KV)]
    for h in range(HEADS_PER_KV):
        s = scores[h]
        if mode != "full":
            s = jnp.where(keep, s, NEG_INF)
        m_old = m_s[h]
        m_new = jnp.maximum(m_old, jnp.max(s, axis=0, keepdims=True))
        alpha = jnp.exp(m_old - m_new)
        p = jnp.exp(s - m_new)
        l_s[h] = alpha * l_s[h] + jnp.sum(p, axis=0, keepdims=True)
        acc_s[h] = alpha * acc_s[h] + _dot(vt, p.astype(BF16))
        m_s[h] = m_new


def _attn_init(m_s, l_s, acc_s):
    m_s[...] = jnp.full(m_s.shape, NEG_INF, F32)
    l_s[...] = jnp.zeros(l_s.shape, F32)
    acc_s[...] = jnp.zeros(acc_s.shape, F32)


def _nsa_kernel(q_ref, kv_ref, kca_ref, vct_ref, gate_ref, qg_ref, kg_ref, selt_ref, kxs_ref,
                kxw_ref, o_ref, ksa_s, kwa_s, vst_s, vwt_s, qt_s, ot_s, m_s, l_s, acc_s):
    tq = TQ
    g = pl.program_id(1)
    qi = pl.program_id(2)
    t0 = qi * tq
    state = (m_s, l_s, acc_s)

    @pl.when(qi == 0)
    def _prep_keys():
        for off, gi, ka_s, vt_s, kx_ref in ((0, 0, ksa_s, vst_s, kxs_ref),
                                            (2 * HEAD_DIM, 1, kwa_s, vwt_s, kxw_ref)):
            kn = _rms(kv_ref[0, :, off:off + HEAD_DIM], kg_ref[gi:gi + 1, :]).astype(BF16)
            ka_s[...] = jnp.concatenate([kn, kx_ref[...]], axis=1)
            for j in range(SEQ // tq):
                kv_t = kv_ref[0, j * tq:(j + 1) * tq, off:off + 2 * HEAD_DIM].T
                vt_s[j] = kv_t[HEAD_DIM:].astype(BF16)

    def by_group(rows, r):
        return jnp.where(g == 0, rows[r:r + 1], rows[r + HEADS_PER_KV:r + HEADS_PER_KV + 1])

    gates_t = jax.nn.sigmoid(gate_ref[0]).T
    q_t = q_ref[0].T
    row32 = lax.broadcasted_iota(jnp.int32, (ROW_SEL - ROW_POS, tq), 0)
    for h in range(HEADS_PER_KV):
        qh = q_t[h * HEAD_DIM:(h + 1) * HEAD_DIM]
        qn = qh * lax.rsqrt(jnp.mean(qh * qh, axis=0, keepdims=True) + EPS) * qg_ref[...]
        slope = jnp.where(g == 0, 2.0 ** -(h + 1), 2.0 ** -(h + 1 + HEADS_PER_KV)).astype(F32)
        qt_s[h, 0:ROW_POS, :] = (qn * QK_SCALE).astype(BF16)
        qt_s[h, ROW_POS:ROW_SEL, :] = jnp.where(row32 < 2, slope, 0.0).astype(BF16)
        qt_s[h, ROW_SEL:AUG, :] = jnp.zeros((N_SEL, tq), BF16)

    blk_end = lax.broadcasted_iota(jnp.int32, (N_CHUNK, tq), 0) * CMP_STRIDE + (CMP_BLOCK - 1)
    valid = t0 + lax.broadcasted_iota(jnp.int32, (N_CHUNK, tq), 1) >= blk_end
    any_valid = (t0 + lax.broadcasted_iota(jnp.int32, (1, tq), 1) >= CMP_BLOCK - 1).astype(F32)
    imp_t = jnp.zeros((N_SEL, tq), F32)
    for h in range(HEADS_PER_KV):
        s = jnp.where(valid, _dot(kca_ref[0, 0], qt_s[h]), NEG_INF)
        e = jnp.exp(s - jnp.max(s, axis=0, keepdims=True))
        p = (e / jnp.sum(e, axis=0, keepdims=True) * any_valid).astype(BF16)
        ot_s[h * HEAD_DIM:(h + 1) * HEAD_DIM, :] = by_group(gates_t, h) * _dot(vct_ref[0, 0], p)
        imp_t = imp_t + _dot(selt_ref[...], p)

    blk = lax.broadcasted_iota(jnp.int32, (N_SEL, tq), 0)
    cur = (t0 + lax.broadcasted_iota(jnp.int32, (N_SEL, tq), 1)) // SEL_BLOCK
    forced = (blk == 0) | (blk == cur) | (blk == cur - 1)
    val = jnp.where(forced, FORCE_SCORE, jnp.where(blk <= cur, imp_t, -FORCE_SCORE))
    rank = jnp.zeros((N_SEL, tq), F32)
    for i in range(N_SEL):
        vi = val[i:i + 1, :]
        rank = rank + jnp.where(blk > i, jnp.where(vi >= val, 1.0, 0.0),
                                jnp.where(vi > val, 1.0, 0.0))
    not_sel = jnp.where(rank < SEL_TOPN, 0.0, NEG_INF).astype(BF16)
    for h in range(HEADS_PER_KV):
        qt_s[h, ROW_SEL:AUG, :] = not_sel

    def finish(gate_row0):
        for h in range(HEADS_PER_KV):
            rows = slice(h * HEAD_DIM, (h + 1) * HEAD_DIM)
            ot_s[rows, :] += by_group(gates_t, gate_row0 + h) * (acc_s[h] / l_s[h])

    _attn_init(*state)

    def slc_body(jt, carry):
        _attn_step(ksa_s, vst_s, qt_s, *state, jt, "full")
        return carry

    lax.fori_loop(0, qi, slc_body, 0)
    _attn_step(ksa_s, vst_s, qt_s, *state, qi, "causal")
    finish(N_HEADS)

    n_back = WINDOW // tq
    _attn_init(*state)
    for back in range(n_back, 0, -1):
        @pl.when(qi >= back)
        def _(back=back):
            _attn_step(kwa_s, vwt_s, qt_s, *state, qi - back, "band" if back == n_back else "full")
    _attn_step(kwa_s, vwt_s, qt_s, *state, qi, "causal")
    finish(2 * N_HEADS)

    o_ref[0] = ot_s[...].T.astype(BF16)


def _nsa(q, kvg, kca, vct, gate, qg_t, kg, selt, kxs, kxw):
    b, t, _ = q.shape
    tq = TQ
    gq = HEADS_PER_KV * HEAD_DIM
    full = lambda a: pl.BlockSpec(a.shape, lambda i, j, k: (0,) * a.ndim)
    return pl.pallas_call(
        _nsa_kernel,
        grid=(b, KV_HEADS, t // tq),
        in_specs=[pl.BlockSpec((1, tq, gq), lambda i, j, k: (i, k, j)),
                  pl.BlockSpec((1, t, KVG_WIDTH), lambda i, j, k: (i, 0, j)),
                  pl.BlockSpec((1, 1, N_CHUNK, AUG), lambda i, j, k: (i, j, 0, 0)),
                  pl.BlockSpec((1, 1, HEAD_DIM, N_CHUNK), lambda i, j, k: (i, j, 0, 0)),
                  pl.BlockSpec((1, tq, GATE_PAD), lambda i, j, k: (i, k, 0)),
                  full(qg_t), full(kg), full(selt), full(kxs), full(kxw)],
        out_specs=pl.BlockSpec((1, tq, gq), lambda i, j, k: (i, k, j)),
        out_shape=jax.ShapeDtypeStruct((b, t, NSA_WIDTH), BF16),
        scratch_shapes=[pltpu.VMEM((t, AUG), BF16), pltpu.VMEM((t, AUG), BF16),
                        pltpu.VMEM((t // tq, HEAD_DIM, tq), BF16),
                        pltpu.VMEM((t // tq, HEAD_DIM, tq), BF16),
                        pltpu.VMEM((HEADS_PER_KV, AUG, tq), BF16),
                        pltpu.VMEM((gq, tq), F32),
                        pltpu.VMEM((HEADS_PER_KV, 1, tq), F32),
                        pltpu.VMEM((HEADS_PER_KV, 1, tq), F32),
                        pltpu.VMEM((HEADS_PER_KV, HEAD_DIM, tq), F32)],
        compiler_params=pltpu.CompilerParams(
            dimension_semantics=("arbitrary", "arbitrary", "arbitrary"),
            vmem_limit_bytes=VMEM_LIMIT),
        name="nsa",
    )(q, kvg, kca, vct, gate, qg_t, kg, selt, kxs, kxw)


def _out_proj_kernel(pool_ref, conv_ref, nsa_ref, x_ref, pbd_ref, pscale_ref, sw_ref, wout_ref,
                     g2_ref, x1_ref, h2_ref, cat_s, ptail_s, ctail_s):
    tm = TM_PROJ
    ti = pl.program_id(1)

    @pl.when(ti == 0)
    def _reset():
        ptail_s[...] = jnp.zeros_like(ptail_s)
        ctail_s[...] = jnp.zeros_like(ctail_s)

    u = pool_ref[0]
    ext = jnp.concatenate([ptail_s[...], u], axis=0)
    sums = []
    acc = ext
    for shift in (1, 2, 4, 8):
        acc = acc + pltpu.roll(acc, shift, 0)
        sums.append(acc[HALO:])
    grp = lax.broadcasted_iota(jnp.int32, (1, POOL_WIDTH), 1) // POOL_GDIM
    wsum = jnp.where(grp == 0, sums[0], jnp.where(grp == 1, sums[1],
                                                  jnp.where(grp == 2, sums[2], sums[3])))
    width = jnp.where(grp == 0, POOL_WINDOWS[0], jnp.where(
        grp == 1, POOL_WINDOWS[1], jnp.where(grp == 2, POOL_WINDOWS[2], POOL_WINDOWS[3])))
    tpos = ti * tm + lax.broadcasted_iota(jnp.int32, (tm, 1), 0) + 1
    cnt = jnp.minimum(tpos, width).astype(F32)
    diff = (wsum / cnt - u).astype(BF16)
    cat_s[:, 0:POOL_WIDTH] = (_dot(diff, pbd_ref[...]) * pscale_ref[...]).astype(BF16)
    ptail_s[...] = u[tm - HALO:]

    cat_s[:, POOL_WIDTH:POOL_WIDTH + NSA_WIDTH] = nsa_ref[0]

    cb = conv_ref[0, :, 0:CONV_WIDTH]
    prod = conv_ref[0, :, CONV_WIDTH:2 * CONV_WIDTH] * conv_ref[0, :, 2 * CONV_WIDTH:3 * CONV_WIDTH]
    pext = jnp.concatenate([ctail_s[...], prod], axis=0)
    conv = (sw_ref[2:3, :] * pext + sw_ref[1:2, :] * pltpu.roll(pext, 1, 0)
            + sw_ref[0:1, :] * pltpu.roll(pext, 2, 0))[HALO:]
    cat_s[:, POOL_WIDTH + NSA_WIDTH:] = (cb * conv).astype(BF16)
    ctail_s[...] = prod[tm - HALO:]

    x1 = x_ref[0] + _dot(cat_s[...], wout_ref[...])
    x1_ref[0] = x1
    h2_ref[0] = _rms(x1, g2_ref[...]).astype(BF16)


def _out_proj(pool, conv, nsa, x, pbd, pscale, sw, wout, g2):
    b, t, d = x.shape
    tm = TM_PROJ
    tile = lambda width: pl.BlockSpec((1, tm, width), lambda i, j: (i, j, 0))
    full = lambda a: pl.BlockSpec(a.shape, lambda i, j: (0,) * a.ndim)
    return pl.pallas_call(
        _out_proj_kernel,
        grid=(b, t // tm),
        in_specs=[tile(POOL_WIDTH), tile(3 * CONV_WIDTH), tile(NSA_WIDTH), tile(d),
                  full(pbd), full(pscale), full(sw), full(wout), full(g2)],
        out_specs=[tile(d), tile(d)],
        out_shape=[jax.ShapeDtypeStruct((b, t, d), F32), jax.ShapeDtypeStruct((b, t, d), BF16)],
        scratch_shapes=[pltpu.VMEM((tm, d), BF16),
                        pltpu.VMEM((HALO, POOL_WIDTH), F32),
                        pltpu.VMEM((HALO, CONV_WIDTH), F32)],
        compiler_params=pltpu.CompilerParams(
            dimension_semantics=("arbitrary", "arbitrary"), vmem_limit_bytes=VMEM_LIMIT),
        name="out_proj",
    )(pool, conv, nsa, x, pbd, pscale, sw, wout, g2)


def _ffn_kernel(h2_ref, x1_ref, wup_ref, cw_ref, wd_ref, o_ref, z_s, atail_s):
    tm = TM_PROJ
    ti = pl.program_id(1)

    @pl.when(ti == 0)
    def _reset():
        atail_s[...] = jnp.zeros_like(atail_s)

    h2 = h2_ref[0]
    for c in range(D_FF // FF_CHUNK):
        lo, hi = c * FF_CHUNK, (c + 1) * FF_CHUNK
        a = _dot(h2, wup_ref[:, lo:hi])
        gate = _dot(h2, wup_ref[:, D_FF + lo:D_FF + hi])
        aext = jnp.concatenate([atail_s[:, lo:hi], a], axis=0)
        conv = (cw_ref[2:3, lo:hi] * aext + cw_ref[1:2, lo:hi] * pltpu.roll(aext, 1, 0)
                + cw_ref[0:1, lo:hi] * pltpu.roll(aext, 2, 0))[HALO:]
        z_s[:, lo:hi] = (jax.nn.silu(conv) * gate).astype(BF16)
        atail_s[:, lo:hi] = a[tm - HALO:]
    o_ref[0] = x1_ref[0] + _dot(z_s[...], wd_ref[...])


def _ffn(h2, x1, wup, cw, wd):
    b, t, d = x1.shape
    tm = TM_PROJ
    tile = pl.BlockSpec((1, tm, d), lambda i, j: (i, j, 0))
    full = lambda a: pl.BlockSpec(a.shape, lambda i, j: (0,) * a.ndim)
    return pl.pallas_call(
        _ffn_kernel,
        grid=(b, t // tm),
        in_specs=[tile, tile, full(wup), full(cw), full(wd)],
        out_specs=tile,
        out_shape=jax.ShapeDtypeStruct((b, t, d), F32),
        scratch_shapes=[pltpu.VMEM((tm, D_FF), BF16), pltpu.VMEM((HALO, D_FF), F32)],
        compiler_params=pltpu.CompilerParams(
            dimension_semantics=("arbitrary", "arbitrary"), vmem_limit_bytes=VMEM_LIMIT),
        name="ffn",
    )(h2, x1, wup, cw, wd)


def _attention_constants():
    c0 = np.arange(N_CMP) * CMP_STRIDE
    s0 = np.arange(N_SEL) * SEL_BLOCK
    ov = (np.minimum(c0[:, None] + CMP_BLOCK, s0[None, :] + SEL_BLOCK)
          - np.maximum(c0[:, None], s0[None, :]))
    frac = np.zeros((N_CHUNK, N_SEL), np.float32)
    frac[:N_CMP] = np.clip(ov, 0, None) / CMP_BLOCK

    def pos_cols(pos, rows):
        x = np.zeros((rows, AUG - HEAD_DIM), np.float32)
        x[:, 0] = pos % POS_SPLIT
        x[:, 1] = pos - pos % POS_SPLIT
        return x

    kpos = np.arange(SEQ)
    kxw = pos_cols(kpos, SEQ)
    kxs = pos_cols(kpos, SEQ)
    kxs[kpos, ROW_SEL - HEAD_DIM + kpos // SEL_BLOCK] = 1.0
    cx = pos_cols(np.arange(N_CHUNK) * CMP_STRIDE + CMP_BLOCK - 1, N_CHUNK)
    return tuple(jnp.asarray(a, BF16) for a in (frac.T, kxs, kxw, cx))


def _layer_weights(w_in, pool_w):
    o_q = POOL_WIDTH
    o_kv = o_q + NSA_WIDTH
    o_gate = o_kv + 6 * KV_WIDTH
    o_conv = o_gate + 3 * N_HEADS
    kv = [w_in[:, o_kv + i * KV_WIDTH:o_kv + (i + 1) * KV_WIDTH] for i in range(6)]
    grp = lambda a, g: a[:, g * HEAD_DIM:(g + 1) * HEAD_DIM]
    kvg = [grp(kv[i], g) for g in range(KV_HEADS) for i in (2, 3, 4, 5)]
    gate = jnp.pad(w_in[:, o_gate:o_conv], ((0, 0), (0, GATE_PAD - 3 * N_HEADS)))
    w = jnp.concatenate([w_in[:, :o_kv]] + kvg + [kv[0], kv[1], w_in[:, o_conv:], gate], axis=1)
    pbd = jax.scipy.linalg.block_diag(*[pool_w[i] for i in range(POOL_GROUPS)])
    return w.astype(BF16), pbd.astype(BF16)


def kernel(x, norm1_g, w_in, pool_w, pool_scale, q_norm_g, k_norm_g, cmp_pos, cmp_w1, cmp_w2,
           sconv_w, w_out, norm2_g, ffn_up, ffn_conv, ffn_down):
    depth = w_in.shape[0]
    b, t, _ = x.shape
    selt, kxs, kxw, cx = _attention_constants()
    half = CMP_STRIDE * HEAD_DIM
    for l in range(depth):
        w, pbd = _layer_weights(w_in[l], pool_w[l])
        pool, q, kvg, kvc, conv, gate = _in_proj(x, norm1_g[l][None, :], w)
        kvc = kvc.reshape(b, t, 2, KV_HEADS, HEAD_DIM).transpose(2, 0, 3, 1, 4)
        kvc = kvc.reshape(2, b, KV_HEADS, N_CHUNK, half)
        pos = cmp_pos[l].reshape(2, 2, 1, half)
        w1 = cmp_w1[l].reshape(2, 2, half, HEAD_DIM).astype(BF16)
        kca, vct = _compress(kvc[0], kvc[1], pos[:, 0], pos[:, 1], w1[:, 0], w1[:, 1],
                             cmp_w2[l].astype(BF16), k_norm_g[l][0:1], cx)
        qg_t = jnp.broadcast_to(q_norm_g[l][:, None], (HEAD_DIM, TQ))
        nsa = _nsa(q, kvg, kca, vct, gate, qg_t, k_norm_g[l][1:3], selt, kxs, kxw)
        x1, h2 = _out_proj(pool, conv, nsa, x, pbd, pool_scale[l][None, :], sconv_w[l],
                           w_out[l].astype(BF16), norm2_g[l][None, :])
        x = _ffn(h2, x1, ffn_up[l].astype(BF16), ffn_conv[l], ffn_down[l].astype(BF16))
    return x
```

```python
import numpy as np
import jax
import jax.numpy as jnp
from jax import lax
from jax.experimental import pallas as pl
from jax.experimental.pallas import tpu as pltpu

D_MODEL = 1024
SEQ = 2048
HEAD_DIM = 64
POOL_WINDOWS = (2, 4, 8, 16)
POOL_GROUPS = 4
POOL_WIDTH = 256
POOL_GDIM = POOL_WIDTH // POOL_GROUPS
NSA_WIDTH = 512
N_HEADS = 8
KV_HEADS = 2
HEADS_PER_KV = N_HEADS // KV_HEADS
KV_WIDTH = KV_HEADS * HEAD_DIM
CONV_WIDTH = 256
CMP_BLOCK = 32
CMP_STRIDE = 16
N_CMP = (SEQ - CMP_BLOCK) // CMP_STRIDE + 1
N_CHUNK = SEQ // CMP_STRIDE
SEL_BLOCK = 64
N_SEL = SEQ // SEL_BLOCK
SEL_TOPN = 8
WINDOW = 512
D_FF = 2816
EPS = 1e-6
NEG_INF = -1e30
FORCE_SCORE = 1e6
QK_SCALE = HEAD_DIM ** -0.5

LANES = 128
GATE_PAD = LANES
KVG_WIDTH = 4 * HEAD_DIM
HALO = 16
AUG = 2 * HEAD_DIM
POS_SPLIT = 256
ROW_POS = HEAD_DIM
ROW_SEL = AUG - N_SEL
V_ROWS = HEAD_DIM + 16

TM_PROJ = 512
TQ = 256
FF_CHUNK = 256
VMEM_LIMIT = 56 * 1024 * 1024

assert WINDOW % TQ == 0 and SEQ % TQ == 0 and TQ % SEL_BLOCK == 0

F32 = jnp.float32
BF16 = jnp.bfloat16


def _rms(x, g):
    return x * lax.rsqrt(jnp.mean(x * x, axis=-1, keepdims=True) + EPS) * g


def _dot(a, b):
    return jnp.dot(a, b, preferred_element_type=F32)


IN_SEGS = (("pool", POOL_WIDTH), ("q", NSA_WIDTH), ("kvg", KV_HEADS * KVG_WIDTH),
           ("kvc", 2 * KV_WIDTH), ("conv", 3 * CONV_WIDTH), ("gate", GATE_PAD))


def _in_proj_kernel(x_ref, g_ref, w_ref, *out_refs):
    h = _rms(x_ref[0], g_ref[...]).astype(BF16)
    off = 0
    for (_, width), o_ref in zip(IN_SEGS, out_refs):
        o_ref[0] = _dot(h, w_ref[:, off:off + width])
        off += width


def _in_proj(x, g, w):
    b, t, d = x.shape
    n = w.shape[1]
    tm = TM_PROJ
    out_shape = [jax.ShapeDtypeStruct((b, t, width), F32) for _, width in IN_SEGS]
    out_specs = [pl.BlockSpec((1, tm, width), lambda i, j: (i, j, 0)) for _, width in IN_SEGS]
    return pl.pallas_call(
        _in_proj_kernel,
        grid=(b, t // tm),
        in_specs=[pl.BlockSpec((1, tm, d), lambda i, j: (i, j, 0)),
                  pl.BlockSpec((1, d), lambda i, j: (0, 0)),
                  pl.BlockSpec((d, n), lambda i, j: (0, 0))],
        out_specs=out_specs,
        out_shape=out_shape,
        compiler_params=pltpu.CompilerParams(
            dimension_semantics=("arbitrary", "arbitrary"), vmem_limit_bytes=VMEM_LIMIT),
        name="in_proj",
    )(x, g, w)


def _compress_kernel(kc_ref, vc_ref, plo_ref, phi_ref, w1lo_ref, w1hi_ref, w2_ref, kg_ref, cx_ref,
                     ko_ref, vo_ref):
    outs = []
    for i, src in enumerate((kc_ref, vc_ref)):
        c = src[0, 0]
        lo = _dot((c + plo_ref[i]).astype(BF16), w1lo_ref[i])
        hi = _dot((c + phi_ref[i]).astype(BF16), w1hi_ref[i])
        hid = jax.nn.gelu(lo + pltpu.roll(hi, N_CHUNK - 1, 0))
        outs.append(_dot(hid.astype(BF16), w2_ref[i]))
    ko_ref[0, 0] = jnp.concatenate([_rms(outs[0], kg_ref[...]).astype(BF16), cx_ref[...]], axis=1)
    v_pad = jnp.concatenate([outs[1], jnp.zeros_like(outs[1])], axis=1)
    vo_ref[0, 0] = v_pad.T[0:HEAD_DIM].astype(BF16)


def _compress(kc, vc, plo, phi, w1lo, w1hi, w2, kg, cx):
    b, g = kc.shape[:2]
    blk = pl.BlockSpec((1, 1, N_CHUNK, CMP_STRIDE * HEAD_DIM), lambda i, j: (i, j, 0, 0))
    full = lambda a: pl.BlockSpec(a.shape, lambda i, j: (0,) * a.ndim)
    return pl.pallas_call(
        _compress_kernel,
        grid=(b, g),
        in_specs=[blk, blk, full(plo), full(phi), full(w1lo), full(w1hi), full(w2), full(kg),
                  full(cx)],
        out_specs=[pl.BlockSpec((1, 1, N_CHUNK, AUG), lambda i, j: (i, j, 0, 0)),
                   pl.BlockSpec((1, 1, HEAD_DIM, N_CHUNK), lambda i, j: (i, j, 0, 0))],
        out_shape=[jax.ShapeDtypeStruct((b, g, N_CHUNK, AUG), BF16),
                   jax.ShapeDtypeStruct((b, g, HEAD_DIM, N_CHUNK), BF16)],
        compiler_params=pltpu.CompilerParams(
            dimension_semantics=("arbitrary", "arbitrary"), vmem_limit_bytes=VMEM_LIMIT),
        name="compress",
    )(kc, vc, plo, phi, w1lo, w1hi, w2, kg, cx)


def _score_tile(ka_s, qt_s, jt, s_ref):
    k0 = pl.multiple_of(jt * TQ, TQ)
    ka = ka_s[pl.ds(k0, TQ), :]
    for h in range(HEADS_PER_KV):
        s_ref[h] = _dot(ka, qt_s[h])


def _softmax_tile(s_ref, vt, m_s, acc_s, mode):
    tq = TQ
    if mode != "full":
        key = lax.broadcasted_iota(jnp.int32, (tq, tq), 0)
        qry = lax.broadcasted_iota(jnp.int32, (tq, tq), 1)
        keep = key <= qry if mode == "causal" else key > qry
    for h in range(HEADS_PER_KV):
        s = s_ref[h]
        if mode != "full":
            s = jnp.where(keep, s, NEG_INF)
        m_old = m_s[h]
        m_new = jnp.maximum(m_old, jnp.max(s, axis=0, keepdims=True))
        alpha = jnp.exp(m_old - m_new)
        p = jnp.exp(s - m_new).astype(BF16)
        acc_s[h] = alpha * acc_s[h] + _dot(vt, p)
        m_s[h] = m_new


def _attn_init(m_s, acc_s):
    m_s[...] = jnp.full(m_s.shape, NEG_INF, F32)
    acc_s[...] = jnp.zeros(acc_s.shape, F32)


def _attn_keep_if(live, m_s, acc_s):
    m_s[...] = jnp.where(live, m_s[...], NEG_INF)
    acc_s[...] = jnp.where(live, acc_s[...], 0.0)


def _nsa_kernel(q_ref, kv_ref, kca_ref, vct_ref, gate_ref, qg_ref, kg_ref, selt_ref, kxs_ref,
                kxw_ref, o_ref, ksa_s, kwa_s, vst_s, vwt_s, qt_s, ot_s, m_s, acc_s, wbuf, sbuf):
    tq = TQ
    g = pl.program_id(1)
    qi = pl.program_id(2)
    t0 = qi * tq
    state = (m_s, acc_s)

    @pl.when(qi == 0)
    def _prep_keys():
        ones_row = jnp.where(lax.broadcasted_iota(jnp.int32, (V_ROWS - HEAD_DIM, tq), 0) == 0,
                             1.0, 0.0).astype(BF16)
        for off, gi, ka_s, vt_s, kx_ref in ((0, 0, ksa_s, vst_s, kxs_ref),
                                            (2 * HEAD_DIM, 1, kwa_s, vwt_s, kxw_ref)):
            kn = _rms(kv_ref[0, :, off:off + HEAD_DIM], kg_ref[gi:gi + 1, :]).astype(BF16)
            ka_s[...] = jnp.concatenate([kn, kx_ref[...]], axis=1)
            for j in range(SEQ // tq):
                kv_t = kv_ref[0, j * tq:(j + 1) * tq, off:off + 2 * HEAD_DIM].T
                vt_s[j, 0:HEAD_DIM, :] = kv_t[HEAD_DIM:].astype(BF16)
                vt_s[j, HEAD_DIM:V_ROWS, :] = ones_row

    def by_group(rows, r):
        return jnp.where(g == 0, rows[r:r + 1], rows[r + HEADS_PER_KV:r + HEADS_PER_KV + 1])

    gates_t = jax.nn.sigmoid(gate_ref[0]).T
    q_t = q_ref[0].T
    row32 = lax.broadcasted_iota(jnp.int32, (ROW_SEL - ROW_POS, tq), 0)
    for h in range(HEADS_PER_KV):
        qh = q_t[h * HEAD_DIM:(h + 1) * HEAD_DIM]
        qn = qh * lax.rsqrt(jnp.mean(qh * qh, axis=0, keepdims=True) + EPS) * qg_ref[...]
        slope = jnp.where(g == 0, 2.0 ** -(h + 1), 2.0 ** -(h + 1 + HEADS_PER_KV)).astype(F32)
        qt_s[h, 0:ROW_POS, :] = (qn * QK_SCALE).astype(BF16)
        qt_s[h, ROW_POS:ROW_SEL, :] = jnp.where(row32 < 2, slope, 0.0).astype(BF16)
        qt_s[h, ROW_SEL:AUG, :] = jnp.zeros((N_SEL, tq), BF16)

    n_back = WINDOW // tq
    win_tiles = [jnp.maximum(qi - back, 0) for back in range(n_back, 0, -1)] + [qi]
    for i, jt in enumerate(win_tiles):
        _score_tile(kwa_s, qt_s, jt, wbuf.at[i])

    blk_end = lax.broadcasted_iota(jnp.int32, (N_CHUNK, tq), 0) * CMP_STRIDE + (CMP_BLOCK - 1)
    valid = t0 + lax.broadcasted_iota(jnp.int32, (N_CHUNK, tq), 1) >= blk_end
    any_valid = (t0 + lax.broadcasted_iota(jnp.int32, (1, tq), 1) >= CMP_BLOCK - 1).astype(F32)
    imp_t = jnp.zeros((N_SEL, tq), F32)
    for h in range(HEADS_PER_KV):
        s = jnp.where(valid, _dot(kca_ref[0, 0], qt_s[h]), NEG_INF)
        e = jnp.exp(s - jnp.max(s, axis=0, keepdims=True))
        p = (e / jnp.sum(e, axis=0, keepdims=True) * any_valid).astype(BF16)
        ot_s[h * HEAD_DIM:(h + 1) * HEAD_DIM, :] = by_group(gates_t, h) * _dot(vct_ref[0, 0], p)
        imp_t = imp_t + _dot(selt_ref[...], p)

    blk = lax.broadcasted_iota(jnp.int32, (N_SEL, tq), 0)
    cur = (t0 + lax.broadcasted_iota(jnp.int32, (N_SEL, tq), 1)) // SEL_BLOCK
    forced = (blk == 0) | (blk == cur) | (blk == cur - 1)
    val = jnp.where(forced, FORCE_SCORE, jnp.where(blk <= cur, imp_t, -FORCE_SCORE))
    rank = jnp.zeros((N_SEL, tq), F32)
    for i in range(N_SEL):
        vi = val[i:i + 1, :]
        rank = rank + jnp.where(blk > i, jnp.where(vi >= val, 1.0, 0.0),
                                jnp.where(vi > val, 1.0, 0.0))
    not_sel = jnp.where(rank < SEL_TOPN, 0.0, NEG_INF).astype(BF16)
    for h in range(HEADS_PER_KV):
        qt_s[h, ROW_SEL:AUG, :] = not_sel

    def finish(gate_row0):
        for h in range(HEADS_PER_KV):
            rows = slice(h * HEAD_DIM, (h + 1) * HEAD_DIM)
            a = acc_s[h]
            out = a[0:HEAD_DIM] / a[HEAD_DIM:HEAD_DIM + 1]
            ot_s[rows, :] += by_group(gates_t, gate_row0 + h) * out

    _score_tile(ksa_s, qt_s, 0, sbuf.at[0])

    _attn_init(*state)
    for i, back in enumerate(range(n_back, 0, -1)):
        _softmax_tile(wbuf.at[i], vwt_s[win_tiles[i]], *state, "band" if back == n_back else "full")
        _attn_keep_if(qi >= back, *state)
    _softmax_tile(wbuf.at[n_back], vwt_s[qi], *state, "causal")
    finish(2 * N_HEADS)

    _attn_init(*state)

    def slc_pair(i, carry):
        jt = 2 * i
        _score_tile(ksa_s, qt_s, jt + 1, sbuf.at[1])
        _softmax_tile(sbuf.at[0], vst_s[jt], *state, "full")
        _score_tile(ksa_s, qt_s, jt + 2, sbuf.at[0])
        _softmax_tile(sbuf.at[1], vst_s[jt + 1], *state, "full")
        return carry

    lax.fori_loop(0, qi // 2, slc_pair, 0)

    @pl.when(qi % 2 == 0)
    def _even_tail():
        _softmax_tile(sbuf.at[0], vst_s[qi], *state, "causal")

    @pl.when(qi % 2 == 1)
    def _odd_tail():
        _score_tile(ksa_s, qt_s, qi, sbuf.at[1])
        _softmax_tile(sbuf.at[0], vst_s[qi - 1], *state, "full")
        _softmax_tile(sbuf.at[1], vst_s[qi], *state, "causal")

    finish(N_HEADS)

    o_ref[0] = ot_s[...].T.astype(BF16)


def _nsa(q, kvg, kca, vct, gate, qg_t, kg, selt, kxs, kxw):
    b, t, _ = q.shape
    tq = TQ
    gq = HEADS_PER_KV * HEAD_DIM
    full = lambda a: pl.BlockSpec(a.shape, lambda i, j, k: (0,) * a.ndim)
    return pl.pallas_call(
        _nsa_kernel,
        grid=(b, KV_HEADS, t // tq),
        in_specs=[pl.BlockSpec((1, tq, gq), lambda i, j, k: (i, k, j)),
                  pl.BlockSpec((1, t, KVG_WIDTH), lambda i, j, k: (i, 0, j)),
                  pl.BlockSpec((1, 1, N_CHUNK, AUG), lambda i, j, k: (i, j, 0, 0)),
                  pl.BlockSpec((1, 1, HEAD_DIM, N_CHUNK), lambda i, j, k: (i, j, 0, 0)),
                  pl.BlockSpec((1, tq, GATE_PAD), lambda i, j, k: (i, k, 0)),
                  full(qg_t), full(kg), full(selt), full(kxs), full(kxw)],
        out_specs=pl.BlockSpec((1, tq, gq), lambda i, j, k: (i, k, j)),
        out_shape=jax.ShapeDtypeStruct((b, t, NSA_WIDTH), BF16),
        scratch_shapes=[pltpu.VMEM((t, AUG), BF16), pltpu.VMEM((t, AUG), BF16),
                        pltpu.VMEM((t // tq, V_ROWS, tq), BF16),
                        pltpu.VMEM((t // tq, V_ROWS, tq), BF16),
                        pltpu.VMEM((HEADS_PER_KV, AUG, tq), BF16),
                        pltpu.VMEM((gq, tq), F32),
                        pltpu.VMEM((HEADS_PER_KV, 1, tq), F32),
                        pltpu.VMEM((HEADS_PER_KV, V_ROWS, tq), F32),
                        pltpu.VMEM((WINDOW // tq + 1, HEADS_PER_KV, tq, tq), F32),
                        pltpu.VMEM((2, HEADS_PER_KV, tq, tq), F32)],
        compiler_params=pltpu.CompilerParams(
            dimension_semantics=("arbitrary", "arbitrary", "arbitrary"),
            vmem_limit_bytes=VMEM_LIMIT),
        name="nsa",
    )(q, kvg, kca, vct, gate, qg_t, kg, selt, kxs, kxw)


def _out_proj_kernel(pool_ref, conv_ref, nsa_ref, x_ref, pbd_ref, pscale_ref, sw_ref, wout_ref,
                     g2_ref, x1_ref, h2_ref, cat_s, ptail_s, ctail_s):
    tm = TM_PROJ
    ti = pl.program_id(1)

    @pl.when(ti == 0)
    def _reset():
        ptail_s[...] = jnp.zeros_like(ptail_s)
        ctail_s[...] = jnp.zeros_like(ctail_s)

    u = pool_ref[0]
    ext = jnp.concatenate([ptail_s[...], u], axis=0)
    sums = []
    acc = ext
    for shift in (1, 2, 4, 8):
        acc = acc + pltpu.roll(acc, shift, 0)
        sums.append(acc[HALO:])
    grp = lax.broadcasted_iota(jnp.int32, (1, POOL_WIDTH), 1) // POOL_GDIM
    wsum = jnp.where(grp == 0, sums[0], jnp.where(grp == 1, sums[1],
                                                  jnp.where(grp == 2, sums[2], sums[3])))
    width = jnp.where(grp == 0, POOL_WINDOWS[0], jnp.where(
        grp == 1, POOL_WINDOWS[1], jnp.where(grp == 2, POOL_WINDOWS[2], POOL_WINDOWS[3])))
    tpos = ti * tm + lax.broadcasted_iota(jnp.int32, (tm, 1), 0) + 1
    cnt = jnp.minimum(tpos, width).astype(F32)
    diff = (wsum / cnt - u).astype(BF16)
    cat_s[:, 0:POOL_WIDTH] = (_dot(diff, pbd_ref[...]) * pscale_ref[...]).astype(BF16)
    ptail_s[...] = u[tm - HALO:]

    cat_s[:, POOL_WIDTH:POOL_WIDTH + NSA_WIDTH] = nsa_ref[0]

    cb = conv_ref[0, :, 0:CONV_WIDTH]
    prod = conv_ref[0, :, CONV_WIDTH:2 * CONV_WIDTH] * conv_ref[0, :, 2 * CONV_WIDTH:3 * CONV_WIDTH]
    pext = jnp.concatenate([ctail_s[...], prod], axis=0)
    conv = (sw_ref[2:3, :] * pext + sw_ref[1:2, :] * pltpu.roll(pext, 1, 0)
            + sw_ref[0:1, :] * pltpu.roll(pext, 2, 0))[HALO:]
    cat_s[:, POOL_WIDTH + NSA_WIDTH:] = (cb * conv).astype(BF16)
    ctail_s[...] = prod[tm - HALO:]

    x1 = x_ref[0] + _dot(cat_s[...], wout_ref[...])
    x1_ref[0] = x1
    h2_ref[0] = _rms(x1, g2_ref[...]).astype(BF16)


def _out_proj(pool, conv, nsa, x, pbd, pscale, sw, wout, g2):
    b, t, d = x.shape
    tm = TM_PROJ
    tile = lambda width: pl.BlockSpec((1, tm, width), lambda i, j: (i, j, 0))
    full = lambda a: pl.BlockSpec(a.shape, lambda i, j: (0,) * a.ndim)
    return pl.pallas_call(
        _out_proj_kernel,
        grid=(b, t // tm),
        in_specs=[tile(POOL_WIDTH), tile(3 * CONV_WIDTH), tile(NSA_WIDTH), tile(d),
                  full(pbd), full(pscale), full(sw), full(wout), full(g2)],
        out_specs=[tile(d), tile(d)],
        out_shape=[jax.ShapeDtypeStruct((b, t, d), F32), jax.ShapeDtypeStruct((b, t, d), BF16)],
        scratch_shapes=[pltpu.VMEM((tm, d), BF16),
                        pltpu.VMEM((HALO, POOL_WIDTH), F32),
                        pltpu.VMEM((HALO, CONV_WIDTH), F32)],
        compiler_params=pltpu.CompilerParams(
            dimension_semantics=("arbitrary", "arbitrary"), vmem_limit_bytes=VMEM_LIMIT),
        name="out_proj",
    )(pool, conv, nsa, x, pbd, pscale, sw, wout, g2)


def _ffn_kernel(h2_ref, x1_ref, wup_ref, cw_ref, wd_ref, o_ref, z_s, atail_s):
    tm = TM_PROJ
    ti = pl.program_id(1)

    @pl.when(ti == 0)
    def _reset():
        atail_s[...] = jnp.zeros_like(atail_s)

    h2 = h2_ref[0]
    for c in range(D_FF // FF_CHUNK):
        lo, hi = c * FF_CHUNK, (c + 1) * FF_CHUNK
        a = _dot(h2, wup_ref[:, lo:hi])
        gate = _dot(h2, wup_ref[:, D_FF + lo:D_FF + hi])
        aext = jnp.concatenate([atail_s[:, lo:hi], a], axis=0)
        conv = (cw_ref[2:3, lo:hi] * aext + cw_ref[1:2, lo:hi] * pltpu.roll(aext, 1, 0)
                + cw_ref[0:1, lo:hi] * pltpu.roll(aext, 2, 0))[HALO:]
        z_s[:, lo:hi] = (jax.nn.silu(conv) * gate).astype(BF16)
        atail_s[:, lo:hi] = a[tm - HALO:]
    o_ref[0] = x1_ref[0] + _dot(z_s[...], wd_ref[...])


def _ffn(h2, x1, wup, cw, wd):
    b, t, d = x1.shape
    tm = TM_PROJ
    tile = pl.BlockSpec((1, tm, d), lambda i, j: (i, j, 0))
    full = lambda a: pl.BlockSpec(a.shape, lambda i, j: (0,) * a.ndim)
    return pl.pallas_call(
        _ffn_kernel,
        grid=(b, t // tm),
        in_specs=[tile, tile, full(wup), full(cw), full(wd)],
        out_specs=tile,
        out_shape=jax.ShapeDtypeStruct((b, t, d), F32),
        scratch_shapes=[pltpu.VMEM((tm, D_FF), BF16), pltpu.VMEM((HALO, D_FF), F32)],
        compiler_params=pltpu.CompilerParams(
            dimension_semantics=("arbitrary", "arbitrary"), vmem_limit_bytes=VMEM_LIMIT),
        name="ffn",
    )(h2, x1, wup, cw, wd)


def _attention_constants():
    c0 = np.arange(N_CMP) * CMP_STRIDE
    s0 = np.arange(N_SEL) * SEL_BLOCK
    ov = (np.minimum(c0[:, None] + CMP_BLOCK, s0[None, :] + SEL_BLOCK)
          - np.maximum(c0[:, None], s0[None, :]))
    frac = np.zeros((N_CHUNK, N_SEL), np.float32)
    frac[:N_CMP] = np.clip(ov, 0, None) / CMP_BLOCK

    def pos_cols(pos, rows):
        x = np.zeros((rows, AUG - HEAD_DIM), np.float32)
        x[:, 0] = pos % POS_SPLIT
        x[:, 1] = pos - pos % POS_SPLIT
        return x

    kpos = np.arange(SEQ)
    kxw = pos_cols(kpos, SEQ)
    kxs = pos_cols(kpos, SEQ)
    kxs[kpos, ROW_SEL - HEAD_DIM + kpos // SEL_BLOCK] = 1.0
    cx = pos_cols(np.arange(N_CHUNK) * CMP_STRIDE + CMP_BLOCK - 1, N_CHUNK)
    return tuple(jnp.asarray(a, BF16) for a in (frac.T, kxs, kxw, cx))


def _layer_weights(w_in, pool_w):
    o_q = POOL_WIDTH
    o_kv = o_q + NSA_WIDTH
    o_gate = o_kv + 6 * KV_WIDTH
    o_conv = o_gate + 3 * N_HEADS
    kv = [w_in[:, o_kv + i * KV_WIDTH:o_kv + (i + 1) * KV_WIDTH] for i in range(6)]
    grp = lambda a, g: a[:, g * HEAD_DIM:(g + 1) * HEAD_DIM]
    kvg = [grp(kv[i], g) for g in range(KV_HEADS) for i in (2, 3, 4, 5)]
    gate = jnp.pad(w_in[:, o_gate:o_conv], ((0, 0), (0, GATE_PAD - 3 * N_HEADS)))
    w = jnp.concatenate([w_in[:, :o_kv]] + kvg + [kv[0], kv[1], w_in[:, o_conv:], gate], axis=1)
    pbd = jax.scipy.linalg.block_diag(*[pool_w[i] for i in range(POOL_GROUPS)])
    return w.astype(BF16), pbd.astype(BF16)


def kernel(x, norm1_g, w_in, pool_w, pool_scale, q_norm_g, k_norm_g, cmp_pos, cmp_w1, cmp_w2,
           sconv_w, w_out, norm2_g, ffn_up, ffn_conv, ffn_down):
    depth = w_in.shape[0]
    b, t, _ = x.shape
    selt, kxs, kxw, cx = _attention_constants()
    half = CMP_STRIDE * HEAD_DIM
    for l in range(depth):
        w, pbd = _layer_weights(w_in[l], pool_w[l])
        pool, q, kvg, kvc, conv, gate = _in_proj(x, norm1_g[l][None, :], w)
        kvc = kvc.reshape(b, t, 2, KV_HEADS, HEAD_DIM).transpose(2, 0, 3, 1, 4)
        kvc = kvc.reshape(2, b, KV_HEADS, N_CHUNK, half)
        pos = cmp_pos[l].reshape(2, 2, 1, half)
        w1 = cmp_w1[l].reshape(2, 2, half, HEAD_DIM).astype(BF16)
        kca, vct = _compress(kvc[0], kvc[1], pos[:, 0], pos[:, 1], w1[:, 0], w1[:, 1],
                             cmp_w2[l].astype(BF16), k_norm_g[l][0:1], cx)
        qg_t = jnp.broadcast_to(q_norm_g[l][:, None], (HEAD_DIM, TQ))
        nsa = _nsa(q, kvg, kca, vct, gate, qg_t, k_norm_g[l][1:3], selt, kxs, kxw)
        x1, h2 = _out_proj(pool, conv, nsa, x, pbd, pool_scale[l][None, :], sconv_w[l],
                           w_out[l].astype(BF16), norm2_g[l][None, :])
        x = _ffn(h2, x1, ffn_up[l].astype(BF16), ffn_conv[l], ffn_down[l].astype(BF16))
    return x
```

```python
import numpy as np
import jax
import jax.numpy as jnp
from jax import lax
from jax.experimental import pallas as pl
from jax.experimental.pallas import tpu as pltpu

D_MODEL = 1024
SEQ = 2048
HEAD_DIM = 64
POOL_WINDOWS = (2, 4, 8, 16)
POOL_GROUPS = 4
POOL_WIDTH = 256
POOL_GDIM = POOL_WIDTH // POOL_GROUPS
NSA_WIDTH = 512
N_HEADS = 8
KV_HEADS = 2
HEADS_PER_KV = N_HEADS // KV_HEADS
KV_WIDTH = KV_HEADS * HEAD_DIM
CONV_WIDTH = 256
CMP_BLOCK = 32
CMP_STRIDE = 16
N_CMP = (SEQ - CMP_BLOCK) // CMP_STRIDE + 1
N_CHUNK = SEQ // CMP_STRIDE
SEL_BLOCK = 64
N_SEL = SEQ // SEL_BLOCK
SEL_TOPN = 8
WINDOW = 512
D_FF = 2816
EPS = 1e-6
NEG_INF = -1e30
FORCE_SCORE = 1e6
QK_SCALE = HEAD_DIM ** -0.5

LANES = 128
GATE_PAD = LANES
KVG_WIDTH = 4 * HEAD_DIM
KVC_WIDTH = 2 * HEAD_DIM
HALO = 16
AUG = 2 * HEAD_DIM
POS_SPLIT = 256
ROW_POS = HEAD_DIM
ROW_SEL = AUG - N_SEL
V_ROWS = HEAD_DIM + 16

TM_PROJ = 512
TQ = 256
FF_CHUNK = 256
VMEM_LIMIT = 56 * 1024 * 1024

assert WINDOW % TQ == 0 and SEQ % TQ == 0 and TQ % SEL_BLOCK == 0

F32 = jnp.float32
BF16 = jnp.bfloat16


def _rms(x, g):
    return x * lax.rsqrt(jnp.mean(x * x, axis=-1, keepdims=True) + EPS) * g


def _dot(a, b):
    return jnp.dot(a, b, preferred_element_type=F32)


def _layer_spec(a, layer):
    zeros = (0,) * (a.ndim - 1)
    return pl.BlockSpec((None,) + a.shape[1:], lambda *_: (layer,) + zeros,
                        pipeline_mode=pl.Buffered(1))


IN_SEGS = (("pool", POOL_WIDTH), ("q", NSA_WIDTH), ("kvg", KV_HEADS * KVG_WIDTH),
           ("kvc", KV_HEADS * KVC_WIDTH), ("conv", 3 * CONV_WIDTH), ("gate", GATE_PAD))


def _in_proj_kernel(x_ref, g_ref, w_ref, *out_refs):
    h = _rms(x_ref[0], g_ref[...]).astype(BF16)
    off = 0
    for (_, width), o_ref in zip(IN_SEGS, out_refs):
        o_ref[0] = _dot(h, w_ref[:, off:off + width])
        off += width


def _in_proj(x, g, w, layer):
    b, t, d = x.shape
    tm = TM_PROJ
    out_shape = [jax.ShapeDtypeStruct((b, t, width), F32) for _, width in IN_SEGS]
    out_specs = [pl.BlockSpec((1, tm, width), lambda i, j: (i, j, 0)) for _, width in IN_SEGS]
    return pl.pallas_call(
        _in_proj_kernel,
        grid=(b, t // tm),
        in_specs=[pl.BlockSpec((1, tm, d), lambda i, j: (i, j, 0)),
                  _layer_spec(g, layer), _layer_spec(w, layer)],
        out_specs=out_specs,
        out_shape=out_shape,
        compiler_params=pltpu.CompilerParams(
            dimension_semantics=("arbitrary", "arbitrary"), vmem_limit_bytes=VMEM_LIMIT),
        name="in_proj",
    )(x, g, w)


def _score_tile(ka_s, qt_s, jt, s_ref):
    k0 = pl.multiple_of(jt * TQ, TQ)
    ka = ka_s[pl.ds(k0, TQ), :]
    for h in range(HEADS_PER_KV):
        s_ref[h] = _dot(ka, qt_s[h])


def _softmax_tile(s_ref, vt, m_s, acc_s, mode):
    tq = TQ
    if mode != "full":
        key = lax.broadcasted_iota(jnp.int32, (tq, tq), 0)
        qry = lax.broadcasted_iota(jnp.int32, (tq, tq), 1)
        keep = key <= qry if mode == "causal" else key > qry
    for h in range(HEADS_PER_KV):
        s = s_ref[h]
        if mode != "full":
            s = jnp.where(keep, s, NEG_INF)
        m_old = m_s[h]
        m_new = jnp.maximum(m_old, jnp.max(s, axis=0, keepdims=True))
        alpha = jnp.exp(m_old - m_new)
        p = jnp.exp(s - m_new).astype(BF16)
        acc_s[h] = alpha * acc_s[h] + _dot(vt, p)
        m_s[h] = m_new


def _attn_init(m_s, acc_s):
    m_s[...] = jnp.full(m_s.shape, NEG_INF, F32)
    acc_s[...] = jnp.zeros(acc_s.shape, F32)


def _attn_keep_if(live, m_s, acc_s):
    m_s[...] = jnp.where(live, m_s[...], NEG_INF)
    acc_s[...] = jnp.where(live, acc_s[...], 0.0)


def _compress_blocks(kvc_ref, plo_ref, phi_ref, wlo_ref, whi_ref, w2_ref, kg_ref, cx_ref,
                     kca_s, vct_s):
    lo = jnp.zeros((N_CHUNK, 2 * HEAD_DIM), F32)
    hi = jnp.zeros((N_CHUNK, 2 * HEAD_DIM), F32)
    for r in range(CMP_STRIDE):
        x = kvc_ref[0, pl.ds(r, N_CHUNK, stride=CMP_STRIDE), :]
        lo = lo + _dot((x + plo_ref[r]).astype(BF16), wlo_ref[r])
        hi = hi + _dot((x + phi_ref[r]).astype(BF16), whi_ref[r])
    hid = jax.nn.gelu(lo + pltpu.roll(hi, N_CHUNK - 1, 0))
    out = _dot(hid.astype(BF16), w2_ref[...])
    kn = _rms(out[:, 0:HEAD_DIM], kg_ref[2:3, :]).astype(BF16)
    kca_s[...] = jnp.concatenate([kn, cx_ref[...]], axis=1)
    vct_s[...] = out.T[HEAD_DIM:].astype(BF16)


def _nsa_kernel(q_ref, kv_ref, kvc_ref, gate_ref, qg_ref, kg_ref, selt_ref, kxs_ref, kxw_ref,
                cx_ref, plo_ref, phi_ref, wlo_ref, whi_ref, w2_ref, o_ref,
                ksa_s, kwa_s, vst_s, vwt_s, kca_s, vct_s, qt_s, ot_s, m_s, acc_s, wbuf, sbuf):
    tq = TQ
    g = pl.program_id(1)
    qi = pl.program_id(2)
    t0 = qi * tq
    state = (m_s, acc_s)

    @pl.when(qi == 0)
    def _prep_keys():
        ones_row = jnp.where(lax.broadcasted_iota(jnp.int32, (V_ROWS - HEAD_DIM, tq), 0) == 0,
                             1.0, 0.0).astype(BF16)
        for off, gi, ka_s, vt_s, kx_ref in ((0, 0, ksa_s, vst_s, kxs_ref),
                                            (2 * HEAD_DIM, 1, kwa_s, vwt_s, kxw_ref)):
            kn = _rms(kv_ref[0, :, off:off + HEAD_DIM], kg_ref[gi:gi + 1, :]).astype(BF16)
            ka_s[...] = jnp.concatenate([kn, kx_ref[...]], axis=1)
            for j in range(SEQ // tq):
                kv_t = kv_ref[0, j * tq:(j + 1) * tq, off:off + 2 * HEAD_DIM].T
                vt_s[j, 0:HEAD_DIM, :] = kv_t[HEAD_DIM:].astype(BF16)
                vt_s[j, HEAD_DIM:V_ROWS, :] = ones_row
        _compress_blocks(kvc_ref, plo_ref, phi_ref, wlo_ref, whi_ref, w2_ref, kg_ref, cx_ref,
                         kca_s, vct_s)

    def by_group(rows, r):
        return jnp.where(g == 0, rows[r:r + 1], rows[r + HEADS_PER_KV:r + HEADS_PER_KV + 1])

    gates_t = jax.nn.sigmoid(gate_ref[0]).T
    q_t = q_ref[0].T
    row32 = lax.broadcasted_iota(jnp.int32, (ROW_SEL - ROW_POS, tq), 0)
    for h in range(HEADS_PER_KV):
        qh = q_t[h * HEAD_DIM:(h + 1) * HEAD_DIM]
        qn = qh * lax.rsqrt(jnp.mean(qh * qh, axis=0, keepdims=True) + EPS) * qg_ref[...]
        slope = jnp.where(g == 0, 2.0 ** -(h + 1), 2.0 ** -(h + 1 + HEADS_PER_KV)).astype(F32)
        qt_s[h, 0:ROW_POS, :] = (qn * QK_SCALE).astype(BF16)
        qt_s[h, ROW_POS:ROW_SEL, :] = jnp.where(row32 < 2, slope, 0.0).astype(BF16)
        qt_s[h, ROW_SEL:AUG, :] = jnp.zeros((N_SEL, tq), BF16)

    n_back = WINDOW // tq
    win_tiles = [jnp.maximum(qi - back, 0) for back in range(n_back, 0, -1)] + [qi]
    for i, jt in enumerate(win_tiles):
        _score_tile(kwa_s, qt_s, jt, wbuf.at[i])

    blk_end = lax.broadcasted_iota(jnp.int32, (N_CHUNK, tq), 0) * CMP_STRIDE + (CMP_BLOCK - 1)
    valid = t0 + lax.broadcasted_iota(jnp.int32, (N_CHUNK, tq), 1) >= blk_end
    any_valid = (t0 + lax.broadcasted_iota(jnp.int32, (1, tq), 1) >= CMP_BLOCK - 1).astype(F32)
    imp_t = jnp.zeros((N_SEL, tq), F32)
    for h in range(HEADS_PER_KV):
        s = jnp.where(valid, _dot(kca_s[...], qt_s[h]), NEG_INF)
        e = jnp.exp(s - jnp.max(s, axis=0, keepdims=True))
        p = (e / jnp.sum(e, axis=0, keepdims=True) * any_valid).astype(BF16)
        ot_s[h * HEAD_DIM:(h + 1) * HEAD_DIM, :] = by_group(gates_t, h) * _dot(vct_s[...], p)
        imp_t = imp_t + _dot(selt_ref[...], p)

    blk = lax.broadcasted_iota(jnp.int32, (N_SEL, tq), 0)
    cur = (t0 + lax.broadcasted_iota(jnp.int32, (N_SEL, tq), 1)) // SEL_BLOCK
    forced = (blk == 0) | (blk == cur) | (blk == cur - 1)
    val = jnp.where(forced, FORCE_SCORE, jnp.where(blk <= cur, imp_t, -FORCE_SCORE))
    rank = jnp.zeros((N_SEL, tq), F32)
    for i in range(N_SEL):
        vi = val[i:i + 1, :]
        rank = rank + jnp.where(blk > i, jnp.where(vi >= val, 1.0, 0.0),
                                jnp.where(vi > val, 1.0, 0.0))
    not_sel = jnp.where(rank < SEL_TOPN, 0.0, NEG_INF).astype(BF16)
    for h in range(HEADS_PER_KV):
        qt_s[h, ROW_SEL:AUG, :] = not_sel

    def finish(gate_row0):
        for h in range(HEADS_PER_KV):
            rows = slice(h * HEAD_DIM, (h + 1) * HEAD_DIM)
            a = acc_s[h]
            out = a[0:HEAD_DIM] / a[HEAD_DIM:HEAD_DIM + 1]
            ot_s[rows, :] += by_group(gates_t, gate_row0 + h) * out

    _score_tile(ksa_s, qt_s, 0, sbuf.at[0])

    _attn_init(*state)
    for i, back in enumerate(range(n_back, 0, -1)):
        mode = "band" if back == n_back else "full"
        _softmax_tile(wbuf.at[i], vwt_s[win_tiles[i]], *state, mode)
        _attn_keep_if(qi >= back, *state)
    _softmax_tile(wbuf.at[n_back], vwt_s[qi], *state, "causal")
    finish(2 * N_HEADS)

    _attn_init(*state)

    def slc_pair(i, carry):
        jt = 2 * i
        _score_tile(ksa_s, qt_s, jt + 1, sbuf.at[1])
        _softmax_tile(sbuf.at[0], vst_s[jt], *state, "full")
        _score_tile(ksa_s, qt_s, jt + 2, sbuf.at[0])
        _softmax_tile(sbuf.at[1], vst_s[jt + 1], *state, "full")
        return carry

    lax.fori_loop(0, qi // 2, slc_pair, 0)

    @pl.when(qi % 2 == 0)
    def _even_tail():
        _softmax_tile(sbuf.at[0], vst_s[qi], *state, "causal")

    @pl.when(qi % 2 == 1)
    def _odd_tail():
        _score_tile(ksa_s, qt_s, qi, sbuf.at[1])
        _softmax_tile(sbuf.at[0], vst_s[qi - 1], *state, "full")
        _softmax_tile(sbuf.at[1], vst_s[qi], *state, "causal")

    finish(N_HEADS)

    o_ref[0] = ot_s[...].T.astype(BF16)


def _nsa(q, kvg, kvc, gate, qg_t, kg, selt, kxs, kxw, cx, plo, phi, wlo, whi, w2, layer):
    b, t, _ = q.shape
    tq = TQ
    gq = HEADS_PER_KV * HEAD_DIM
    full = lambda a: pl.BlockSpec(a.shape, lambda i, j, k: (0,) * a.ndim)
    per_layer = lambda a: _layer_spec(a, layer)
    return pl.pallas_call(
        _nsa_kernel,
        grid=(b, KV_HEADS, t // tq),
        in_specs=[pl.BlockSpec((1, tq, gq), lambda i, j, k: (i, k, j)),
                  pl.BlockSpec((1, t, KVG_WIDTH), lambda i, j, k: (i, 0, j)),
                  pl.BlockSpec((1, t, KVC_WIDTH), lambda i, j, k: (i, 0, j)),
                  pl.BlockSpec((1, tq, GATE_PAD), lambda i, j, k: (i, k, 0)),
                  per_layer(qg_t), per_layer(kg), full(selt), full(kxs), full(kxw), full(cx),
                  per_layer(plo), per_layer(phi), per_layer(wlo), per_layer(whi),
                  per_layer(w2)],
        out_specs=pl.BlockSpec((1, tq, gq), lambda i, j, k: (i, k, j)),
        out_shape=jax.ShapeDtypeStruct((b, t, NSA_WIDTH), BF16),
        scratch_shapes=[pltpu.VMEM((t, AUG), BF16), pltpu.VMEM((t, AUG), BF16),
                        pltpu.VMEM((t // tq, V_ROWS, tq), BF16),
                        pltpu.VMEM((t // tq, V_ROWS, tq), BF16),
                        pltpu.VMEM((N_CHUNK, AUG), BF16), pltpu.VMEM((HEAD_DIM, N_CHUNK), BF16),
                        pltpu.VMEM((HEADS_PER_KV, AUG, tq), BF16),
                        pltpu.VMEM((gq, tq), F32),
                        pltpu.VMEM((HEADS_PER_KV, 1, tq), F32),
                        pltpu.VMEM((HEADS_PER_KV, V_ROWS, tq), F32),
                        pltpu.VMEM((WINDOW // tq + 1, HEADS_PER_KV, tq, tq), F32),
                        pltpu.VMEM((2, HEADS_PER_KV, tq, tq), F32)],
        compiler_params=pltpu.CompilerParams(
            dimension_semantics=("arbitrary", "arbitrary", "arbitrary"),
            vmem_limit_bytes=VMEM_LIMIT),
        name="nsa",
    )(q, kvg, kvc, gate, qg_t, kg, selt, kxs, kxw, cx, plo, phi, wlo, whi, w2)


def _mix_ffn_kernel(pool_ref, conv_ref, nsa_ref, x_ref, pbd_ref, pscale_ref, sw_ref, wout_ref,
                    g2_ref, wup_ref, cw_ref, wd_ref, o_ref,
                    cat_s, h2_s, z_s, ptail_s, ctail_s, atail_s):
    tm = TM_PROJ
    ti = pl.program_id(1)

    @pl.when(ti == 0)
    def _reset():
        ptail_s[...] = jnp.zeros_like(ptail_s)
        ctail_s[...] = jnp.zeros_like(ctail_s)
        atail_s[...] = jnp.zeros_like(atail_s)

    u = pool_ref[0]
    ext = jnp.concatenate([ptail_s[...], u], axis=0)
    sums = []
    acc = ext
    for shift in (1, 2, 4, 8):
        acc = acc + pltpu.roll(acc, shift, 0)
        sums.append(acc[HALO:])
    grp = lax.broadcasted_iota(jnp.int32, (1, POOL_WIDTH), 1) // POOL_GDIM
    wsum = jnp.where(grp == 0, sums[0], jnp.where(grp == 1, sums[1],
                                                  jnp.where(grp == 2, sums[2], sums[3])))
    width = jnp.where(grp == 0, POOL_WINDOWS[0], jnp.where(
        grp == 1, POOL_WINDOWS[1], jnp.where(grp == 2, POOL_WINDOWS[2], POOL_WINDOWS[3])))
    tpos = ti * tm + lax.broadcasted_iota(jnp.int32, (tm, 1), 0) + 1
    cnt = jnp.minimum(tpos, width).astype(F32)
    diff = (wsum / cnt - u).astype(BF16)
    cat_s[:, 0:POOL_WIDTH] = (_dot(diff, pbd_ref[...]) * pscale_ref[...]).astype(BF16)
    ptail_s[...] = u[tm - HALO:]

    cat_s[:, POOL_WIDTH:POOL_WIDTH + NSA_WIDTH] = nsa_ref[0]

    cb = conv_ref[0, :, 0:CONV_WIDTH]
    prod = conv_ref[0, :, CONV_WIDTH:2 * CONV_WIDTH] * conv_ref[0, :, 2 * CONV_WIDTH:3 * CONV_WIDTH]
    pext = jnp.concatenate([ctail_s[...], prod], axis=0)
    conv = (sw_ref[2:3, :] * pext + sw_ref[1:2, :] * pltpu.roll(pext, 1, 0)
            + sw_ref[0:1, :] * pltpu.roll(pext, 2, 0))[HALO:]
    cat_s[:, POOL_WIDTH + NSA_WIDTH:] = (cb * conv).astype(BF16)
    ctail_s[...] = prod[tm - HALO:]

    x1 = x_ref[0] + _dot(cat_s[...], wout_ref[...])
    o_ref[0] = x1
    h2_s[...] = _rms(x1, g2_ref[...]).astype(BF16)

    for c in range(D_FF // FF_CHUNK):
        lo, hi = c * FF_CHUNK, (c + 1) * FF_CHUNK
        a = _dot(h2_s[...], wup_ref[:, lo:hi])
        gate = _dot(h2_s[...], wup_ref[:, D_FF + lo:D_FF + hi])
        aext = jnp.concatenate([atail_s[:, lo:hi], a], axis=0)
        conv = (cw_ref[2:3, lo:hi] * aext + cw_ref[1:2, lo:hi] * pltpu.roll(aext, 1, 0)
                + cw_ref[0:1, lo:hi] * pltpu.roll(aext, 2, 0))[HALO:]
        z_s[:, lo:hi] = (jax.nn.silu(conv) * gate).astype(BF16)
        atail_s[:, lo:hi] = a[tm - HALO:]
    o_ref[0] += _dot(z_s[...], wd_ref[...])


def _mix_ffn(pool, conv, nsa, x, pbd, pscale, sw, wout, g2, wup, cw, wd, layer):
    b, t, d = x.shape
    tm = TM_PROJ
    tile = lambda width: pl.BlockSpec((1, tm, width), lambda i, j: (i, j, 0))
    per_layer = lambda a: _layer_spec(a, layer)
    return pl.pallas_call(
        _mix_ffn_kernel,
        grid=(b, t // tm),
        in_specs=[tile(POOL_WIDTH), tile(3 * CONV_WIDTH), tile(NSA_WIDTH), tile(d),
                  per_layer(pbd), per_layer(pscale), per_layer(sw), per_layer(wout),
                  per_layer(g2), per_layer(wup), per_layer(cw), per_layer(wd)],
        out_specs=tile(d),
        out_shape=jax.ShapeDtypeStruct((b, t, d), F32),
        scratch_shapes=[pltpu.VMEM((tm, d), BF16), pltpu.VMEM((tm, d), BF16),
                        pltpu.VMEM((tm, D_FF), BF16),
                        pltpu.VMEM((HALO, POOL_WIDTH), F32),
                        pltpu.VMEM((HALO, CONV_WIDTH), F32),
                        pltpu.VMEM((HALO, D_FF), F32)],
        compiler_params=pltpu.CompilerParams(
            dimension_semantics=("arbitrary", "arbitrary"), vmem_limit_bytes=VMEM_LIMIT),
        name="mix_ffn",
    )(pool, conv, nsa, x, pbd, pscale, sw, wout, g2, wup, cw, wd)


def _attention_constants():
    c0 = np.arange(N_CMP) * CMP_STRIDE
    s0 = np.arange(N_SEL) * SEL_BLOCK
    ov = (np.minimum(c0[:, None] + CMP_BLOCK, s0[None, :] + SEL_BLOCK)
          - np.maximum(c0[:, None], s0[None, :]))
    frac = np.zeros((N_CHUNK, N_SEL), np.float32)
    frac[:N_CMP] = np.clip(ov, 0, None) / CMP_BLOCK

    def pos_cols(pos, rows):
        x = np.zeros((rows, AUG - HEAD_DIM), np.float32)
        x[:, 0] = pos % POS_SPLIT
        x[:, 1] = pos - pos % POS_SPLIT
        return x

    kpos = np.arange(SEQ)
    kxw = pos_cols(kpos, SEQ)
    kxs = pos_cols(kpos, SEQ)
    kxs[kpos, ROW_SEL - HEAD_DIM + kpos // SEL_BLOCK] = 1.0
    cx = pos_cols(np.arange(N_CHUNK) * CMP_STRIDE + CMP_BLOCK - 1, N_CHUNK)
    return tuple(jnp.asarray(a, BF16) for a in (frac.T, kxs, kxw, cx))


def _block_diag(blocks):
    k = len(blocks)
    rows = [jnp.concatenate([blk if j == i else jnp.zeros_like(blk) for j in range(k)], axis=-1)
            for i, blk in enumerate(blocks)]
    return jnp.concatenate(rows, axis=-2)


def _in_proj_weights(w_in):
    o_q = POOL_WIDTH
    o_kv = o_q + NSA_WIDTH
    o_gate = o_kv + 6 * KV_WIDTH
    o_conv = o_gate + 3 * N_HEADS
    kv = [w_in[..., o_kv + i * KV_WIDTH:o_kv + (i + 1) * KV_WIDTH] for i in range(6)]
    grp = lambda a, g: a[..., g * HEAD_DIM:(g + 1) * HEAD_DIM]
    kvg = [grp(kv[i], g) for g in range(KV_HEADS) for i in (2, 3, 4, 5)]
    kvc = [grp(kv[i], g) for g in range(KV_HEADS) for i in (0, 1)]
    gate = jnp.pad(w_in[..., o_gate:o_conv], ((0, 0), (0, 0), (0, GATE_PAD - 3 * N_HEADS)))
    cols = [w_in[..., :o_kv]] + kvg + kvc + [w_in[..., o_conv:], gate]
    return jnp.concatenate(cols, axis=-1).astype(BF16)


def _compress_weights(pos, w1, w2):
    depth = pos.shape[0]
    pos = jnp.concatenate([pos[:, 0], pos[:, 1]], axis=-1)[:, :, None, :]
    w1 = w1.reshape(depth, 2, CMP_BLOCK, HEAD_DIM, HEAD_DIM)
    w1 = _block_diag([w1[:, 0], w1[:, 1]]).astype(BF16)
    w2 = _block_diag([w2[:, 0], w2[:, 1]]).astype(BF16)
    return pos[:, :CMP_STRIDE], pos[:, CMP_STRIDE:], w1[:, :CMP_STRIDE], w1[:, CMP_STRIDE:], w2


def kernel(x, norm1_g, w_in, pool_w, pool_scale, q_norm_g, k_norm_g, cmp_pos, cmp_w1, cmp_w2,
           sconv_w, w_out, norm2_g, ffn_up, ffn_conv, ffn_down):
    depth = w_in.shape[0]
    selt, kxs, kxw, cx = _attention_constants()
    w = _in_proj_weights(w_in)
    pbd = _block_diag([pool_w[:, i] for i in range(POOL_GROUPS)]).astype(BF16)
    plo, phi, wlo, whi, w2 = _compress_weights(cmp_pos, cmp_w1, cmp_w2)
    qg_t = jnp.broadcast_to(q_norm_g[:, :, None], (depth, HEAD_DIM, TQ))
    kg = jnp.concatenate([k_norm_g[:, 1:3], k_norm_g[:, 0:1]], axis=1)
    g1, g2, pscale = norm1_g[:, None, :], norm2_g[:, None, :], pool_scale[:, None, :]
    wout, wup, wd = w_out.astype(BF16), ffn_up.astype(BF16), ffn_down.astype(BF16)
    for l in range(depth):
        pool, q, kvg, kvc, conv, gate = _in_proj(x, g1, w, l)
        nsa = _nsa(q, kvg, kvc, gate, qg_t, kg, selt, kxs, kxw, cx, plo, phi, wlo, whi, w2, l)
        x = _mix_ffn(pool, conv, nsa, x, pbd, pscale, sconv_w, wout, g2, wup, ffn_conv, wd, l)
    return x
```

```python
import numpy as np
import jax
import jax.numpy as jnp
from jax import lax
from jax.experimental import pallas as pl
from jax.experimental.pallas import tpu as pltpu

D_MODEL = 1024
SEQ = 2048
HEAD_DIM = 64
POOL_WINDOWS = (2, 4, 8, 16)
POOL_GROUPS = 4
POOL_WIDTH = 256
POOL_GDIM = POOL_WIDTH // POOL_GROUPS
NSA_WIDTH = 512
N_HEADS = 8
KV_HEADS = 2
HEADS_PER_KV = N_HEADS // KV_HEADS
KV_WIDTH = KV_HEADS * HEAD_DIM
CONV_WIDTH = 256
CMP_BLOCK = 32
CMP_STRIDE = 16
N_CMP = (SEQ - CMP_BLOCK) // CMP_STRIDE + 1
N_CHUNK = SEQ // CMP_STRIDE
SEL_BLOCK = 64
N_SEL = SEQ // SEL_BLOCK
SEL_TOPN = 8
WINDOW = 512
D_FF = 2816
EPS = 1e-6
NEG_INF = -1e30
FORCE_SCORE = 1e6
QK_SCALE = HEAD_DIM ** -0.5

LANES = 128
GATE_PAD = LANES
KVG_WIDTH = 4 * HEAD_DIM
KVC_WIDTH = 2 * HEAD_DIM
HALO = 16
AUG = 2 * HEAD_DIM
POS_SPLIT = 256
ROW_POS = HEAD_DIM
ROW_SEL = AUG - N_SEL
V_ROWS = HEAD_DIM + 16

TM_PROJ = 512
TQ = 256
FF_CHUNK = 256
VMEM_LIMIT = 56 * 1024 * 1024

assert WINDOW % TQ == 0 and SEQ % TQ == 0 and TQ % SEL_BLOCK == 0

F32 = jnp.float32
BF16 = jnp.bfloat16


def _rms(x, g):
    return x * lax.rsqrt(jnp.mean(x * x, axis=-1, keepdims=True) + EPS) * g


def _dot(a, b):
    return jnp.dot(a, b, preferred_element_type=F32)


def _layer_spec(a, layer):
    zeros = (0,) * (a.ndim - 1)
    return pl.BlockSpec((None,) + a.shape[1:], lambda *_: (layer,) + zeros,
                        pipeline_mode=pl.Buffered(1))


IN_SEGS = (("pool", POOL_WIDTH), ("q", NSA_WIDTH), ("kvg", KV_HEADS * KVG_WIDTH),
           ("kvc", KV_HEADS * KVC_WIDTH), ("conv", 3 * CONV_WIDTH), ("gate", GATE_PAD))


def _in_proj_kernel(x_ref, g_ref, w_ref, *out_refs):
    h = _rms(x_ref[0], g_ref[...]).astype(BF16)
    off = 0
    for (_, width), o_ref in zip(IN_SEGS, out_refs):
        o_ref[0] = _dot(h, w_ref[:, off:off + width])
        off += width


def _in_proj(x, g, w, layer):
    b, t, d = x.shape
    tm = TM_PROJ
    out_shape = [jax.ShapeDtypeStruct((b, t, width), F32) for _, width in IN_SEGS]
    out_specs = [pl.BlockSpec((1, tm, width), lambda i, j: (i, j, 0)) for _, width in IN_SEGS]
    return pl.pallas_call(
        _in_proj_kernel,
        grid=(b, t // tm),
        in_specs=[pl.BlockSpec((1, tm, d), lambda i, j: (i, j, 0)),
                  _layer_spec(g, layer), _layer_spec(w, layer)],
        out_specs=out_specs,
        out_shape=out_shape,
        compiler_params=pltpu.CompilerParams(
            dimension_semantics=("arbitrary", "arbitrary"), vmem_limit_bytes=VMEM_LIMIT),
        name="in_proj",
    )(x, g, w)


def _score_tile(ka_s, qt_s, jt, s_ref):
    k0 = pl.multiple_of(jt * TQ, TQ)
    ka = ka_s[pl.ds(k0, TQ), :]
    for h in range(HEADS_PER_KV):
        s_ref[h] = _dot(ka, qt_s[h])


def _softmax_tile(s_ref, vt, m_s, acc_s, mode):
    tq = TQ
    if mode != "full":
        key = lax.broadcasted_iota(jnp.int32, (tq, tq), 0)
        qry = lax.broadcasted_iota(jnp.int32, (tq, tq), 1)
        keep = key <= qry if mode == "causal" else key > qry
    for h in range(HEADS_PER_KV):
        s = s_ref[h]
        if mode != "full":
            s = jnp.where(keep, s, NEG_INF)
        m_old = m_s[h]
        m_new = jnp.maximum(m_old, jnp.max(s, axis=0, keepdims=True))
        alpha = jnp.exp(m_old - m_new)
        p = jnp.exp(s - m_new).astype(BF16)
        acc_s[h] = alpha * acc_s[h] + _dot(vt, p)
        m_s[h] = m_new


def _attn_init(m_s, acc_s):
    m_s[...] = jnp.full(m_s.shape, NEG_INF, F32)
    acc_s[...] = jnp.zeros(acc_s.shape, F32)


def _attn_keep_if(live, m_s, acc_s):
    m_s[...] = jnp.where(live, m_s[...], NEG_INF)
    acc_s[...] = jnp.where(live, acc_s[...], 0.0)


def _compress_blocks(kvc_ref, plo_ref, phi_ref, wlo_ref, whi_ref, w2_ref, kg_ref, cx_ref,
                     kca_s, vct_s):
    lo = jnp.zeros((N_CHUNK, 2 * HEAD_DIM), F32)
    hi = jnp.zeros((N_CHUNK, 2 * HEAD_DIM), F32)
    for r in range(CMP_STRIDE):
        x = kvc_ref[0, pl.ds(r, N_CHUNK, stride=CMP_STRIDE), :]
        lo = lo + _dot((x + plo_ref[r]).astype(BF16), wlo_ref[r])
        hi = hi + _dot((x + phi_ref[r]).astype(BF16), whi_ref[r])
    hid = jax.nn.gelu(lo + pltpu.roll(hi, N_CHUNK - 1, 0))
    out = _dot(hid.astype(BF16), w2_ref[...])
    kn = _rms(out[:, 0:HEAD_DIM], kg_ref[2:3, :]).astype(BF16)
    kca_s[...] = jnp.concatenate([kn, cx_ref[...]], axis=1)
    vct_s[...] = out.T[HEAD_DIM:].astype(BF16)


def _nsa_kernel(q_ref, kv_ref, kvc_ref, gate_ref, qg_ref, kg_ref, selt_ref, kxs_ref, kxw_ref,
                cx_ref, plo_ref, phi_ref, wlo_ref, whi_ref, w2_ref, o_ref,
                ksa_s, kwa_s, vst_s, vwt_s, kca_s, vct_s, qt_s, ot_s, m_s, acc_s, wbuf, sbuf):
    tq = TQ
    g = pl.program_id(1)
    state = (m_s, acc_s)

    def prep_keys():
        ones_row = jnp.where(lax.broadcasted_iota(jnp.int32, (V_ROWS - HEAD_DIM, tq), 0) == 0,
                             1.0, 0.0).astype(BF16)
        for off, gi, ka_s, vt_s, kx_ref in ((0, 0, ksa_s, vst_s, kxs_ref),
                                            (2 * HEAD_DIM, 1, kwa_s, vwt_s, kxw_ref)):
            kn = _rms(kv_ref[0, :, off:off + HEAD_DIM], kg_ref[gi:gi + 1, :]).astype(BF16)
            ka_s[...] = jnp.concatenate([kn, kx_ref[...]], axis=1)
            for j in range(SEQ // tq):
                kv_t = kv_ref[0, j * tq:(j + 1) * tq, off:off + 2 * HEAD_DIM].T
                vt_s[j, 0:HEAD_DIM, :] = kv_t[HEAD_DIM:].astype(BF16)
                vt_s[j, HEAD_DIM:V_ROWS, :] = ones_row
        _compress_blocks(kvc_ref, plo_ref, phi_ref, wlo_ref, whi_ref, w2_ref, kg_ref, cx_ref,
                         kca_s, vct_s)

    prep_keys()

    def query_tile(qi, carry):
        t0 = pl.multiple_of(qi * tq, tq)
        rows_q = pl.ds(t0, tq)

        def by_group(rows, r):
            return jnp.where(g == 0, rows[r:r + 1], rows[r + HEADS_PER_KV:r + HEADS_PER_KV + 1])

        gates_t = jax.nn.sigmoid(gate_ref[0, rows_q, :]).T
        q_t = q_ref[0, rows_q, :].T
        row32 = lax.broadcasted_iota(jnp.int32, (ROW_SEL - ROW_POS, tq), 0)
        for h in range(HEADS_PER_KV):
            qh = q_t[h * HEAD_DIM:(h + 1) * HEAD_DIM]
            qn = qh * lax.rsqrt(jnp.mean(qh * qh, axis=0, keepdims=True) + EPS) * qg_ref[...]
            slope = jnp.where(g == 0, 2.0 ** -(h + 1), 2.0 ** -(h + 1 + HEADS_PER_KV)).astype(F32)
            qt_s[h, 0:ROW_POS, :] = (qn * QK_SCALE).astype(BF16)
            qt_s[h, ROW_POS:ROW_SEL, :] = jnp.where(row32 < 2, slope, 0.0).astype(BF16)
            qt_s[h, ROW_SEL:AUG, :] = jnp.zeros((N_SEL, tq), BF16)

        n_back = WINDOW // tq
        win_tiles = [jnp.maximum(qi - back, 0) for back in range(n_back, 0, -1)] + [qi]
        for i, jt in enumerate(win_tiles):
            _score_tile(kwa_s, qt_s, jt, wbuf.at[i])

        blk_end = lax.broadcasted_iota(jnp.int32, (N_CHUNK, tq), 0) * CMP_STRIDE + (CMP_BLOCK - 1)
        valid = t0 + lax.broadcasted_iota(jnp.int32, (N_CHUNK, tq), 1) >= blk_end
        any_valid = (t0 + lax.broadcasted_iota(jnp.int32, (1, tq), 1) >= CMP_BLOCK - 1).astype(F32)
        imp_t = jnp.zeros((N_SEL, tq), F32)
        for h in range(HEADS_PER_KV):
            s = jnp.where(valid, _dot(kca_s[...], qt_s[h]), NEG_INF)
            e = jnp.exp(s - jnp.max(s, axis=0, keepdims=True))
            p = (e / jnp.sum(e, axis=0, keepdims=True) * any_valid).astype(BF16)
            ot_s[h * HEAD_DIM:(h + 1) * HEAD_DIM, :] = by_group(gates_t, h) * _dot(vct_s[...], p)
            imp_t = imp_t + _dot(selt_ref[...], p)

        blk = lax.broadcasted_iota(jnp.int32, (N_SEL, tq), 0)
        cur = (t0 + lax.broadcasted_iota(jnp.int32, (N_SEL, tq), 1)) // SEL_BLOCK
        forced = (blk == 0) | (blk == cur) | (blk == cur - 1)
        val = jnp.where(forced, FORCE_SCORE, jnp.where(blk <= cur, imp_t, -FORCE_SCORE))
        rank = jnp.zeros((N_SEL, tq), F32)
        for i in range(N_SEL):
            vi = val[i:i + 1, :]
            rank = rank + jnp.where(blk > i, jnp.where(vi >= val, 1.0, 0.0),
                                    jnp.where(vi > val, 1.0, 0.0))
        not_sel = jnp.where(rank < SEL_TOPN, 0.0, NEG_INF).astype(BF16)
        for h in range(HEADS_PER_KV):
            qt_s[h, ROW_SEL:AUG, :] = not_sel

        def finish(gate_row0):
            for h in range(HEADS_PER_KV):
                rows = slice(h * HEAD_DIM, (h + 1) * HEAD_DIM)
                a = acc_s[h]
                out = a[0:HEAD_DIM] / a[HEAD_DIM:HEAD_DIM + 1]
                ot_s[rows, :] += by_group(gates_t, gate_row0 + h) * out

        _score_tile(ksa_s, qt_s, 0, sbuf.at[0])

        _attn_init(*state)
        for i, back in enumerate(range(n_back, 0, -1)):
            mode = "band" if back == n_back else "full"
            _softmax_tile(wbuf.at[i], vwt_s[win_tiles[i]], *state, mode)
            _attn_keep_if(qi >= back, *state)
        _softmax_tile(wbuf.at[n_back], vwt_s[qi], *state, "causal")
        finish(2 * N_HEADS)

        _attn_init(*state)

        def slc_pair(i, carry):
            jt = 2 * i
            _score_tile(ksa_s, qt_s, jt + 1, sbuf.at[1])
            _softmax_tile(sbuf.at[0], vst_s[jt], *state, "full")
            _score_tile(ksa_s, qt_s, jt + 2, sbuf.at[0])
            _softmax_tile(sbuf.at[1], vst_s[jt + 1], *state, "full")
            return carry

        lax.fori_loop(0, qi // 2, slc_pair, 0)

        @pl.when(qi % 2 == 0)
        def _even_tail():
            _softmax_tile(sbuf.at[0], vst_s[qi], *state, "causal")

        @pl.when(qi % 2 == 1)
        def _odd_tail():
            _score_tile(ksa_s, qt_s, qi, sbuf.at[1])
            _softmax_tile(sbuf.at[0], vst_s[qi - 1], *state, "full")
            _softmax_tile(sbuf.at[1], vst_s[qi], *state, "causal")

        finish(N_HEADS)

        o_ref[0, rows_q, :] = ot_s[...].T.astype(BF16)
        return carry

    lax.fori_loop(0, SEQ // tq, query_tile, 0)


def _nsa(q, kvg, kvc, gate, qg_t, kg, selt, kxs, kxw, cx, plo, phi, wlo, whi, w2, layer):
    b, t, _ = q.shape
    tq = TQ
    gq = HEADS_PER_KV * HEAD_DIM
    full = lambda a: pl.BlockSpec(a.shape, lambda i, j: (0,) * a.ndim)
    per_layer = lambda a: _layer_spec(a, layer)
    return pl.pallas_call(
        _nsa_kernel,
        grid=(b, KV_HEADS),
        in_specs=[pl.BlockSpec((1, t, gq), lambda i, j: (i, 0, j)),
                  pl.BlockSpec((1, t, KVG_WIDTH), lambda i, j: (i, 0, j)),
                  pl.BlockSpec((1, t, KVC_WIDTH), lambda i, j: (i, 0, j)),
                  pl.BlockSpec((1, t, GATE_PAD), lambda i, j: (i, 0, 0)),
                  per_layer(qg_t), per_layer(kg), full(selt), full(kxs), full(kxw), full(cx),
                  per_layer(plo), per_layer(phi), per_layer(wlo), per_layer(whi),
                  per_layer(w2)],
        out_specs=pl.BlockSpec((1, t, gq), lambda i, j: (i, 0, j)),
        out_shape=jax.ShapeDtypeStruct((b, t, NSA_WIDTH), BF16),
        scratch_shapes=[pltpu.VMEM((t, AUG), BF16), pltpu.VMEM((t, AUG), BF16),
                        pltpu.VMEM((t // tq, V_ROWS, tq), BF16),
                        pltpu.VMEM((t // tq, V_ROWS, tq), BF16),
                        pltpu.VMEM((N_CHUNK, AUG), BF16), pltpu.VMEM((HEAD_DIM, N_CHUNK), BF16),
                        pltpu.VMEM((HEADS_PER_KV, AUG, tq), BF16),
                        pltpu.VMEM((gq, tq), F32),
                        pltpu.VMEM((HEADS_PER_KV, 1, tq), F32),
                        pltpu.VMEM((HEADS_PER_KV, V_ROWS, tq), F32),
                        pltpu.VMEM((WINDOW // tq + 1, HEADS_PER_KV, tq, tq), F32),
                        pltpu.VMEM((2, HEADS_PER_KV, tq, tq), F32)],
        compiler_params=pltpu.CompilerParams(
            dimension_semantics=("arbitrary", "arbitrary"), vmem_limit_bytes=VMEM_LIMIT),
        name="nsa",
    )(q, kvg, kvc, gate, qg_t, kg, selt, kxs, kxw, cx, plo, phi, wlo, whi, w2)


def _mix_ffn_kernel(pool_ref, conv_ref, nsa_ref, x_ref, pbd_ref, pscale_ref, sw_ref, wout_ref,
                    g2_ref, wup_ref, cw_ref, wd_ref, o_ref,
                    cat_s, h2_s, z_s, ptail_s, ctail_s, atail_s):
    tm = TM_PROJ
    ti = pl.program_id(1)

    @pl.when(ti == 0)
    def _reset():
        ptail_s[...] = jnp.zeros_like(ptail_s)
        ctail_s[...] = jnp.zeros_like(ctail_s)
        atail_s[...] = jnp.zeros_like(atail_s)

    u = pool_ref[0]
    ext = jnp.concatenate([ptail_s[...], u], axis=0)
    sums = []
    acc = ext
    for shift in (1, 2, 4, 8):
        acc = acc + pltpu.roll(acc, shift, 0)
        sums.append(acc[HALO:])
    grp = lax.broadcasted_iota(jnp.int32, (1, POOL_WIDTH), 1) // POOL_GDIM
    wsum = jnp.where(grp == 0, sums[0], jnp.where(grp == 1, sums[1],
                                                  jnp.where(grp == 2, sums[2], sums[3])))
    width = jnp.where(grp == 0, POOL_WINDOWS[0], jnp.where(
        grp == 1, POOL_WINDOWS[1], jnp.where(grp == 2, POOL_WINDOWS[2], POOL_WINDOWS[3])))
    tpos = ti * tm + lax.broadcasted_iota(jnp.int32, (tm, 1), 0) + 1
    cnt = jnp.minimum(tpos, width).astype(F32)
    diff = (wsum / cnt - u).astype(BF16)
    cat_s[:, 0:POOL_WIDTH] = (_dot(diff, pbd_ref[...]) * pscale_ref[...]).astype(BF16)
    ptail_s[...] = u[tm - HALO:]

    cat_s[:, POOL_WIDTH:POOL_WIDTH + NSA_WIDTH] = nsa_ref[0]

    cb = conv_ref[0, :, 0:CONV_WIDTH]
    prod = conv_ref[0, :, CONV_WIDTH:2 * CONV_WIDTH] * conv_ref[0, :, 2 * CONV_WIDTH:3 * CONV_WIDTH]
    pext = jnp.concatenate([ctail_s[...], prod], axis=0)
    conv = (sw_ref[2:3, :] * pext + sw_ref[1:2, :] * pltpu.roll(pext, 1, 0)
            + sw_ref[0:1, :] * pltpu.roll(pext, 2, 0))[HALO:]
    cat_s[:, POOL_WIDTH + NSA_WIDTH:] = (cb * conv).astype(BF16)
    ctail_s[...] = prod[tm - HALO:]

    x1 = x_ref[0] + _dot(cat_s[...], wout_ref[...])
    o_ref[0] = x1
    h2_s[...] = _rms(x1, g2_ref[...]).astype(BF16)

    for c in range(D_FF // FF_CHUNK):
        lo, hi = c * FF_CHUNK, (c + 1) * FF_CHUNK
        a = _dot(h2_s[...], wup_ref[:, lo:hi])
        gate = _dot(h2_s[...], wup_ref[:, D_FF + lo:D_FF + hi])
        aext = jnp.concatenate([atail_s[:, lo:hi], a], axis=0)
        conv = (cw_ref[2:3, lo:hi] * aext + cw_ref[1:2, lo:hi] * pltpu.roll(aext, 1, 0)
                + cw_ref[0:1, lo:hi] * pltpu.roll(aext, 2, 0))[HALO:]
        z_s[:, lo:hi] = (jax.nn.silu(conv) * gate).astype(BF16)
        atail_s[:, lo:hi] = a[tm - HALO:]
    o_ref[0] += _dot(z_s[...], wd_ref[...])


def _mix_ffn(pool, conv, nsa, x, pbd, pscale, sw, wout, g2, wup, cw, wd, layer):
    b, t, d = x.shape
    tm = TM_PROJ
    tile = lambda width: pl.BlockSpec((1, tm, width), lambda i, j: (i, j, 0))
    per_layer = lambda a: _layer_spec(a, layer)
    return pl.pallas_call(
        _mix_ffn_kernel,
        grid=(b, t // tm),
        in_specs=[tile(POOL_WIDTH), tile(3 * CONV_WIDTH), tile(NSA_WIDTH), tile(d),
                  per_layer(pbd), per_layer(pscale), per_layer(sw), per_layer(wout),
                  per_layer(g2), per_layer(wup), per_layer(cw), per_layer(wd)],
        out_specs=tile(d),
        out_shape=jax.ShapeDtypeStruct((b, t, d), F32),
        scratch_shapes=[pltpu.VMEM((tm, d), BF16), pltpu.VMEM((tm, d), BF16),
                        pltpu.VMEM((tm, D_FF), BF16),
                        pltpu.VMEM((HALO, POOL_WIDTH), F32),
                        pltpu.VMEM((HALO, CONV_WIDTH), F32),
                        pltpu.VMEM((HALO, D_FF), F32)],
        compiler_params=pltpu.CompilerParams(
            dimension_semantics=("arbitrary", "arbitrary"), vmem_limit_bytes=VMEM_LIMIT),
        name="mix_ffn",
    )(pool, conv, nsa, x, pbd, pscale, sw, wout, g2, wup, cw, wd)


def _attention_constants():
    c0 = np.arange(N_CMP) * CMP_STRIDE
    s0 = np.arange(N_SEL) * SEL_BLOCK
    ov = (np.minimum(c0[:, None] + CMP_BLOCK, s0[None, :] + SEL_BLOCK)
          - np.maximum(c0[:, None], s0[None, :]))
    frac = np.zeros((N_CHUNK, N_SEL), np.float32)
    frac[:N_CMP] = np.clip(ov, 0, None) / CMP_BLOCK

    def pos_cols(pos, rows):
        x = np.zeros((rows, AUG - HEAD_DIM), np.float32)
        x[:, 0] = pos % POS_SPLIT
        x[:, 1] = pos - pos % POS_SPLIT
        return x

    kpos = np.arange(SEQ)
    kxw = pos_cols(kpos, SEQ)
    kxs = pos_cols(kpos, SEQ)
    kxs[kpos, ROW_SEL - HEAD_DIM + kpos // SEL_BLOCK] = 1.0
    cx = pos_cols(np.arange(N_CHUNK) * CMP_STRIDE + CMP_BLOCK - 1, N_CHUNK)
    return tuple(jnp.asarray(a, BF16) for a in (frac.T, kxs, kxw, cx))


def _block_diag(blocks):
    k = len(blocks)
    rows = [jnp.concatenate([blk if j == i else jnp.zeros_like(blk) for j in range(k)], axis=-1)
            for i, blk in enumerate(blocks)]
    return jnp.concatenate(rows, axis=-2)


def _in_proj_weights(w_in):
    o_q = POOL_WIDTH
    o_kv = o_q + NSA_WIDTH
    o_gate = o_kv + 6 * KV_WIDTH
    o_conv = o_gate + 3 * N_HEADS
    kv = [w_in[..., o_kv + i * KV_WIDTH:o_kv + (i + 1) * KV_WIDTH] for i in range(6)]
    grp = lambda a, g: a[..., g * HEAD_DIM:(g + 1) * HEAD_DIM]
    kvg = [grp(kv[i], g) for g in range(KV_HEADS) for i in (2, 3, 4, 5)]
    kvc = [grp(kv[i], g) for g in range(KV_HEADS) for i in (0, 1)]
    gate = jnp.pad(w_in[..., o_gate:o_conv], ((0, 0), (0, 0), (0, GATE_PAD - 3 * N_HEADS)))
    cols = [w_in[..., :o_kv]] + kvg + kvc + [w_in[..., o_conv:], gate]
    return jnp.concatenate(cols, axis=-1).astype(BF16)


def _compress_weights(pos, w1, w2):
    depth = pos.shape[0]
    pos = jnp.concatenate([pos[:, 0], pos[:, 1]], axis=-1)[:, :, None, :]
    w1 = w1.reshape(depth, 2, CMP_BLOCK, HEAD_DIM, HEAD_DIM)
    w1 = _block_diag([w1[:, 0], w1[:, 1]]).astype(BF16)
    w2 = _block_diag([w2[:, 0], w2[:, 1]]).astype(BF16)
    return pos[:, :CMP_STRIDE], pos[:, CMP_STRIDE:], w1[:, :CMP_STRIDE], w1[:, CMP_STRIDE:], w2


def kernel(x, norm1_g, w_in, pool_w, pool_scale, q_norm_g, k_norm_g, cmp_pos, cmp_w1, cmp_w2,
           sconv_w, w_out, norm2_g, ffn_up, ffn_conv, ffn_down):
    depth = w_in.shape[0]
    selt, kxs, kxw, cx = _attention_constants()
    w = _in_proj_weights(w_in)
    pbd = _block_diag([pool_w[:, i] for i in range(POOL_GROUPS)]).astype(BF16)
    plo, phi, wlo, whi, w2 = _compress_weights(cmp_pos, cmp_w1, cmp_w2)
    qg_t = jnp.broadcast_to(q_norm_g[:, :, None], (depth, HEAD_DIM, TQ))
    kg = jnp.concatenate([k_norm_g[:, 1:3], k_norm_g[:, 0:1]], axis=1)
    g1, g2, pscale = norm1_g[:, None, :], norm2_g[:, None, :], pool_scale[:, None, :]
    wout, wup, wd = w_out.astype(BF16), ffn_up.astype(BF16), ffn_down.astype(BF16)
    for l in range(depth):
        pool, q, kvg, kvc, conv, gate = _in_proj(x, g1, w, l)
        nsa = _nsa(q, kvg, kvc, gate, qg_t, kg, selt, kxs, kxw, cx, plo, phi, wlo, whi, w2, l)
        x = _mix_ffn(pool, conv, nsa, x, pbd, pscale, sconv_w, wout, g2, wup, ffn_conv, wd, l)
    return x
```

```python
import numpy as np
import jax
import jax.numpy as jnp
from jax import lax
from jax.experimental import pallas as pl
from jax.experimental.pallas import tpu as pltpu

D_MODEL = 1024
SEQ = 2048
HEAD_DIM = 64
POOL_WINDOWS = (2, 4, 8, 16)
POOL_GROUPS = 4
POOL_WIDTH = 256
POOL_GDIM = POOL_WIDTH // POOL_GROUPS
NSA_WIDTH = 512
N_HEADS = 8
KV_HEADS = 2
HEADS_PER_KV = N_HEADS // KV_HEADS
KV_WIDTH = KV_HEADS * HEAD_DIM
CONV_WIDTH = 256
CMP_BLOCK = 32
CMP_STRIDE = 16
N_CMP = (SEQ - CMP_BLOCK) // CMP_STRIDE + 1
N_CHUNK = SEQ // CMP_STRIDE
SEL_BLOCK = 64
N_SEL = SEQ // SEL_BLOCK
SEL_TOPN = 8
WINDOW = 512
D_FF = 2816
EPS = 1e-6
NEG_INF = -1e30
FORCE_SCORE = 1e6
QK_SCALE = HEAD_DIM ** -0.5

LANES = 128
GATE_PAD = LANES
KVG_WIDTH = 4 * HEAD_DIM
KVC_WIDTH = 2 * HEAD_DIM
HALO = 16
AUG = 2 * HEAD_DIM
POS_SPLIT = 256
ROW_POS = HEAD_DIM
ROW_SEL = AUG - N_SEL
V_ROWS = HEAD_DIM + 16

TM_PROJ = 512
TQ = 256
FF_CHUNK = 256
VMEM_LIMIT = 56 * 1024 * 1024

assert WINDOW % TQ == 0 and SEQ % TQ == 0 and TQ % SEL_BLOCK == 0

F32 = jnp.float32
BF16 = jnp.bfloat16


def _rms(x, g):
    return x * lax.rsqrt(jnp.mean(x * x, axis=-1, keepdims=True) + EPS) * g


def _dot(a, b):
    return jnp.dot(a, b, preferred_element_type=F32)


def _layer_spec(a, layer):
    zeros = (0,) * (a.ndim - 1)
    return pl.BlockSpec((None,) + a.shape[1:], lambda *_: (layer,) + zeros,
                        pipeline_mode=pl.Buffered(1))


IN_SEGS = (("pool", POOL_WIDTH), ("q", NSA_WIDTH), ("kvg", KV_HEADS * KVG_WIDTH),
           ("kvc", KV_HEADS * KVC_WIDTH), ("conv", 3 * CONV_WIDTH), ("gate", GATE_PAD))


def _in_proj_kernel(x_ref, g_ref, w_ref, *out_refs):
    h = _rms(x_ref[0], g_ref[...]).astype(BF16)
    off = 0
    for (_, width), o_ref in zip(IN_SEGS, out_refs):
        o_ref[0] = _dot(h, w_ref[:, off:off + width])
        off += width


def _in_proj(x, g, w, layer):
    b, t, d = x.shape
    tm = TM_PROJ
    out_shape = [jax.ShapeDtypeStruct((b, t, width), F32) for _, width in IN_SEGS]
    out_specs = [pl.BlockSpec((1, tm, width), lambda i, j: (i, j, 0)) for _, width in IN_SEGS]
    return pl.pallas_call(
        _in_proj_kernel,
        grid=(b, t // tm),
        in_specs=[pl.BlockSpec((1, tm, d), lambda i, j: (i, j, 0)),
                  _layer_spec(g, layer), _layer_spec(w, layer)],
        out_specs=out_specs,
        out_shape=out_shape,
        compiler_params=pltpu.CompilerParams(
            dimension_semantics=("arbitrary", "arbitrary"), vmem_limit_bytes=VMEM_LIMIT),
        name="in_proj",
    )(x, g, w)


def _score_tile(ka_s, qt_s, jt, s_ref):
    k0 = pl.multiple_of(jt * TQ, TQ)
    ka = ka_s[pl.ds(k0, TQ), :]
    for h in range(HEADS_PER_KV):
        s_ref[h] = _dot(ka, qt_s[h])


def _softmax_tile(s_ref, vt, m_s, acc_s, mode):
    tq = TQ
    if mode != "full":
        key = lax.broadcasted_iota(jnp.int32, (tq, tq), 0)
        qry = lax.broadcasted_iota(jnp.int32, (tq, tq), 1)
        keep = key <= qry if mode == "causal" else key > qry
    for h in range(HEADS_PER_KV):
        s = s_ref[h]
        if mode != "full":
            s = jnp.where(keep, s, NEG_INF)
        m_old = m_s[h]
        m_new = jnp.maximum(m_old, jnp.max(s, axis=0, keepdims=True))
        alpha = jnp.exp(m_old - m_new)
        p = jnp.exp(s - m_new).astype(BF16)
        acc_s[h] = alpha * acc_s[h] + _dot(vt, p)
        m_s[h] = m_new


def _attn_init(m_s, acc_s):
    m_s[...] = jnp.full(m_s.shape, NEG_INF, F32)
    acc_s[...] = jnp.zeros(acc_s.shape, F32)


def _attn_keep_if(live, m_s, acc_s):
    m_s[...] = jnp.where(live, m_s[...], NEG_INF)
    acc_s[...] = jnp.where(live, acc_s[...], 0.0)


def _compress_blocks(kvc_ref, plo_ref, phi_ref, wlo_ref, whi_ref, w2_ref, kg_ref, cx_ref,
                     kca_s, vct_s):
    lo = jnp.zeros((N_CHUNK, 2 * HEAD_DIM), F32)
    hi = jnp.zeros((N_CHUNK, 2 * HEAD_DIM), F32)
    for r in range(CMP_STRIDE):
        x = kvc_ref[0, pl.ds(r, N_CHUNK, stride=CMP_STRIDE), :]
        lo = lo + _dot((x + plo_ref[r]).astype(BF16), wlo_ref[r])
        hi = hi + _dot((x + phi_ref[r]).astype(BF16), whi_ref[r])
    hid = jax.nn.gelu(lo + pltpu.roll(hi, N_CHUNK - 1, 0))
    out = _dot(hid.astype(BF16), w2_ref[...])
    kn = _rms(out[:, 0:HEAD_DIM], kg_ref[2:3, :]).astype(BF16)
    kca_s[...] = jnp.concatenate([kn, cx_ref[...]], axis=1)
    vct_s[...] = out.T[HEAD_DIM:].astype(BF16)


def _nsa_kernel(q_ref, kv_ref, kvc_ref, gate_ref, qg_ref, kg_ref, selt_ref, kxs_ref, kxw_ref,
                cx_ref, plo_ref, phi_ref, wlo_ref, whi_ref, w2_ref, o_ref,
                ksa_s, kwa_s, vst_s, vwt_s, kca_s, vct_s, qt_s, ot2_s, m_s, acc_s, wbuf, sbuf):
    tq = TQ
    g = pl.program_id(1)
    state = (m_s, acc_s)

    def prep_keys():
        ones_row = jnp.where(lax.broadcasted_iota(jnp.int32, (V_ROWS - HEAD_DIM, tq), 0) == 0,
                             1.0, 0.0).astype(BF16)
        for off, gi, ka_s, vt_s, kx_ref in ((0, 0, ksa_s, vst_s, kxs_ref),
                                            (2 * HEAD_DIM, 1, kwa_s, vwt_s, kxw_ref)):
            kn = _rms(kv_ref[0, :, off:off + HEAD_DIM], kg_ref[gi:gi + 1, :]).astype(BF16)
            ka_s[...] = jnp.concatenate([kn, kx_ref[...]], axis=1)
            for j in range(SEQ // tq):
                kv_t = kv_ref[0, j * tq:(j + 1) * tq, off:off + 2 * HEAD_DIM].T
                vt_s[j, 0:HEAD_DIM, :] = kv_t[HEAD_DIM:].astype(BF16)
                vt_s[j, HEAD_DIM:V_ROWS, :] = ones_row
        _compress_blocks(kvc_ref, plo_ref, phi_ref, wlo_ref, whi_ref, w2_ref, kg_ref, cx_ref,
                         kca_s, vct_s)

    prep_keys()

    def query_tile(qi, ot_s):
        t0 = pl.multiple_of(qi * tq, tq)
        rows_q = pl.ds(t0, tq)

        def by_group(rows, r):
            return jnp.where(g == 0, rows[r:r + 1], rows[r + HEADS_PER_KV:r + HEADS_PER_KV + 1])

        gates_t = jax.nn.sigmoid(gate_ref[0, rows_q, :]).T
        q_t = q_ref[0, rows_q, :].T
        row32 = lax.broadcasted_iota(jnp.int32, (ROW_SEL - ROW_POS, tq), 0)
        for h in range(HEADS_PER_KV):
            qh = q_t[h * HEAD_DIM:(h + 1) * HEAD_DIM]
            qn = qh * lax.rsqrt(jnp.mean(qh * qh, axis=0, keepdims=True) + EPS) * qg_ref[...]
            slope = jnp.where(g == 0, 2.0 ** -(h + 1), 2.0 ** -(h + 1 + HEADS_PER_KV)).astype(F32)
            qt_s[h, 0:ROW_POS, :] = (qn * QK_SCALE).astype(BF16)
            qt_s[h, ROW_POS:ROW_SEL, :] = jnp.where(row32 < 2, slope, 0.0).astype(BF16)
            qt_s[h, ROW_SEL:AUG, :] = jnp.zeros((N_SEL, tq), BF16)

        n_back = WINDOW // tq
        win_tiles = [jnp.maximum(qi - back, 0) for back in range(n_back, 0, -1)] + [qi]

        blk_end = lax.broadcasted_iota(jnp.int32, (N_CHUNK, tq), 0) * CMP_STRIDE + (CMP_BLOCK - 1)
        valid = t0 + lax.broadcasted_iota(jnp.int32, (N_CHUNK, tq), 1) >= blk_end
        any_valid = (t0 + lax.broadcasted_iota(jnp.int32, (1, tq), 1) >= CMP_BLOCK - 1).astype(F32)
        imp_t = jnp.zeros((N_SEL, tq), F32)
        cmp_scores = [_dot(kca_s[...], qt_s[h]) for h in range(HEADS_PER_KV)]
        _score_tile(kwa_s, qt_s, win_tiles[0], wbuf.at[0])
        for h in range(HEADS_PER_KV):
            s = jnp.where(valid, cmp_scores[h], NEG_INF)
            e = jnp.exp(s - jnp.max(s, axis=0, keepdims=True))
            p = (e / jnp.sum(e, axis=0, keepdims=True) * any_valid).astype(BF16)
            ot_s[h * HEAD_DIM:(h + 1) * HEAD_DIM, :] = by_group(gates_t, h) * _dot(vct_s[...], p)
            imp_t = imp_t + _dot(selt_ref[...], p)

        for i in range(1, n_back):
            _score_tile(kwa_s, qt_s, win_tiles[i], wbuf.at[i])

        blk = lax.broadcasted_iota(jnp.int32, (N_SEL, tq), 0)
        cur = (t0 + lax.broadcasted_iota(jnp.int32, (N_SEL, tq), 1)) // SEL_BLOCK
        forced = (blk == 0) | (blk == cur) | (blk == cur - 1)
        val = jnp.where(forced, FORCE_SCORE, jnp.where(blk <= cur, imp_t, -FORCE_SCORE))
        rank = jnp.zeros((N_SEL, tq), F32)
        for i in range(N_SEL):
            vi = val[i:i + 1, :]
            rank = rank + jnp.where(blk > i, jnp.where(vi >= val, 1.0, 0.0),
                                    jnp.where(vi > val, 1.0, 0.0))
        not_sel = jnp.where(rank < SEL_TOPN, 0.0, NEG_INF).astype(BF16)
        for h in range(HEADS_PER_KV):
            qt_s[h, ROW_SEL:AUG, :] = not_sel

        def finish(gate_row0):
            for h in range(HEADS_PER_KV):
                rows = slice(h * HEAD_DIM, (h + 1) * HEAD_DIM)
                a = acc_s[h]
                out = a[0:HEAD_DIM] / a[HEAD_DIM:HEAD_DIM + 1]
                ot_s[rows, :] += by_group(gates_t, gate_row0 + h) * out

        _score_tile(ksa_s, qt_s, 0, sbuf.at[0])

        _attn_init(*state)
        for i, back in enumerate(range(n_back, 0, -1)):
            mode = "band" if back == n_back else "full"
            _softmax_tile(wbuf.at[i], vwt_s[win_tiles[i]], *state, mode)
            _attn_keep_if(qi >= back, *state)
            if i == 0:
                _score_tile(kwa_s, qt_s, qi, wbuf.at[n_back])
        _softmax_tile(wbuf.at[n_back], vwt_s[qi], *state, "causal")
        finish(2 * N_HEADS)

        _attn_init(*state)

        def slc_pair(i, carry):
            jt = 2 * i
            _score_tile(ksa_s, qt_s, jt + 1, sbuf.at[1])
            _softmax_tile(sbuf.at[0], vst_s[jt], *state, "full")
            _score_tile(ksa_s, qt_s, jt + 2, sbuf.at[0])
            _softmax_tile(sbuf.at[1], vst_s[jt + 1], *state, "full")
            return carry

        lax.fori_loop(0, qi // 2, slc_pair, 0)

        @pl.when(qi % 2 == 0)
        def _even_tail():
            _softmax_tile(sbuf.at[0], vst_s[qi], *state, "causal")

        @pl.when(qi % 2 == 1)
        def _odd_tail():
            _score_tile(ksa_s, qt_s, qi, sbuf.at[1])
            _softmax_tile(sbuf.at[0], vst_s[qi - 1], *state, "full")
            _softmax_tile(sbuf.at[1], vst_s[qi], *state, "causal")

        finish(N_HEADS)

        o_ref[0, rows_q, :] = ot_s[...].T.astype(BF16)

    def query_tile_pair(i, carry):
        query_tile(2 * i, ot2_s.at[0])
        query_tile(2 * i + 1, ot2_s.at[1])
        return carry

    lax.fori_loop(0, SEQ // tq // 2, query_tile_pair, 0)


def _nsa(q, kvg, kvc, gate, qg_t, kg, selt, kxs, kxw, cx, plo, phi, wlo, whi, w2, layer):
    b, t, _ = q.shape
    tq = TQ
    gq = HEADS_PER_KV * HEAD_DIM
    full = lambda a: pl.BlockSpec(a.shape, lambda i, j: (0,) * a.ndim)
    per_layer = lambda a: _layer_spec(a, layer)
    return pl.pallas_call(
        _nsa_kernel,
        grid=(b, KV_HEADS),
        in_specs=[pl.BlockSpec((1, t, gq), lambda i, j: (i, 0, j)),
                  pl.BlockSpec((1, t, KVG_WIDTH), lambda i, j: (i, 0, j)),
                  pl.BlockSpec((1, t, KVC_WIDTH), lambda i, j: (i, 0, j)),
                  pl.BlockSpec((1, t, GATE_PAD), lambda i, j: (i, 0, 0)),
                  per_layer(qg_t), per_layer(kg), full(selt), full(kxs), full(kxw), full(cx),
                  per_layer(plo), per_layer(phi), per_layer(wlo), per_layer(whi),
                  per_layer(w2)],
        out_specs=pl.BlockSpec((1, t, gq), lambda i, j: (i, 0, j)),
        out_shape=jax.ShapeDtypeStruct((b, t, NSA_WIDTH), BF16),
        scratch_shapes=[pltpu.VMEM((t, AUG), BF16), pltpu.VMEM((t, AUG), BF16),
                        pltpu.VMEM((t // tq, V_ROWS, tq), BF16),
                        pltpu.VMEM((t // tq, V_ROWS, tq), BF16),
                        pltpu.VMEM((N_CHUNK, AUG), BF16), pltpu.VMEM((HEAD_DIM, N_CHUNK), BF16),
                        pltpu.VMEM((HEADS_PER_KV, AUG, tq), BF16),
                        pltpu.VMEM((2, gq, tq), F32),
                        pltpu.VMEM((HEADS_PER_KV, 1, tq), F32),
                        pltpu.VMEM((HEADS_PER_KV, V_ROWS, tq), F32),
                        pltpu.VMEM((WINDOW // tq + 1, HEADS_PER_KV, tq, tq), F32),
                        pltpu.VMEM((2, HEADS_PER_KV, tq, tq), F32)],
        compiler_params=pltpu.CompilerParams(
            dimension_semantics=("arbitrary", "arbitrary"), vmem_limit_bytes=VMEM_LIMIT),
        name="nsa",
    )(q, kvg, kvc, gate, qg_t, kg, selt, kxs, kxw, cx, plo, phi, wlo, whi, w2)


def _mix_ffn_kernel(pool_ref, conv_ref, nsa_ref, x_ref, pbd_ref, pscale_ref, sw_ref, wout_ref,
                    g2_ref, wup_ref, cw_ref, wd_ref, o_ref,
                    cat_s, h2_s, z_s, ptail_s, ctail_s, atail_s):
    tm = TM_PROJ
    ti = pl.program_id(1)

    @pl.when(ti == 0)
    def _reset():
        ptail_s[...] = jnp.zeros_like(ptail_s)
        ctail_s[...] = jnp.zeros_like(ctail_s)
        atail_s[...] = jnp.zeros_like(atail_s)

    u = pool_ref[0]
    ext = jnp.concatenate([ptail_s[...], u], axis=0)
    sums = []
    acc = ext
    for shift in (1, 2, 4, 8):
        acc = acc + pltpu.roll(acc, shift, 0)
        sums.append(acc[HALO:])
    grp = lax.broadcasted_iota(jnp.int32, (1, POOL_WIDTH), 1) // POOL_GDIM
    wsum = jnp.where(grp == 0, sums[0], jnp.where(grp == 1, sums[1],
                                                  jnp.where(grp == 2, sums[2], sums[3])))
    width = jnp.where(grp == 0, POOL_WINDOWS[0], jnp.where(
        grp == 1, POOL_WINDOWS[1], jnp.where(grp == 2, POOL_WINDOWS[2], POOL_WINDOWS[3])))
    tpos = ti * tm + lax.broadcasted_iota(jnp.int32, (tm, 1), 0) + 1
    cnt = jnp.minimum(tpos, width).astype(F32)
    diff = (wsum / cnt - u).astype(BF16)
    cat_s[:, 0:POOL_WIDTH] = (_dot(diff, pbd_ref[...]) * pscale_ref[...]).astype(BF16)
    ptail_s[...] = u[tm - HALO:]

    cat_s[:, POOL_WIDTH:POOL_WIDTH + NSA_WIDTH] = nsa_ref[0]

    cb = conv_ref[0, :, 0:CONV_WIDTH]
    prod = conv_ref[0, :, CONV_WIDTH:2 * CONV_WIDTH] * conv_ref[0, :, 2 * CONV_WIDTH:3 * CONV_WIDTH]
    pext = jnp.concatenate([ctail_s[...], prod], axis=0)
    conv = (sw_ref[2:3, :] * pext + sw_ref[1:2, :] * pltpu.roll(pext, 1, 0)
            + sw_ref[0:1, :] * pltpu.roll(pext, 2, 0))[HALO:]
    cat_s[:, POOL_WIDTH + NSA_WIDTH:] = (cb * conv).astype(BF16)
    ctail_s[...] = prod[tm - HALO:]

    x1 = x_ref[0] + _dot(cat_s[...], wout_ref[...])
    o_ref[0] = x1
    h2_s[...] = _rms(x1, g2_ref[...]).astype(BF16)

    for c in range(D_FF // FF_CHUNK):
        lo, hi = c * FF_CHUNK, (c + 1) * FF_CHUNK
        a = _dot(h2_s[...], wup_ref[:, lo:hi])
        gate = _dot(h2_s[...], wup_ref[:, D_FF + lo:D_FF + hi])
        aext = jnp.concatenate([atail_s[:, lo:hi], a], axis=0)
        conv = (cw_ref[2:3, lo:hi] * aext + cw_ref[1:2, lo:hi] * pltpu.roll(aext, 1, 0)
                + cw_ref[0:1, lo:hi] * pltpu.roll(aext, 2, 0))[HALO:]
        z_s[:, lo:hi] = (jax.nn.silu(conv) * gate).astype(BF16)
        atail_s[:, lo:hi] = a[tm - HALO:]
    o_ref[0] += _dot(z_s[...], wd_ref[...])


def _mix_ffn(pool, conv, nsa, x, pbd, pscale, sw, wout, g2, wup, cw, wd, layer):
    b, t, d = x.shape
    tm = TM_PROJ
    tile = lambda width: pl.BlockSpec((1, tm, width), lambda i, j: (i, j, 0))
    per_layer = lambda a: _layer_spec(a, layer)
    return pl.pallas_call(
        _mix_ffn_kernel,
        grid=(b, t // tm),
        in_specs=[tile(POOL_WIDTH), tile(3 * CONV_WIDTH), tile(NSA_WIDTH), tile(d),
                  per_layer(pbd), per_layer(pscale), per_layer(sw), per_layer(wout),
                  per_layer(g2), per_layer(wup), per_layer(cw), per_layer(wd)],
        out_specs=tile(d),
        out_shape=jax.ShapeDtypeStruct((b, t, d), F32),
        scratch_shapes=[pltpu.VMEM((tm, d), BF16), pltpu.VMEM((tm, d), BF16),
                        pltpu.VMEM((tm, D_FF), BF16),
                        pltpu.VMEM((HALO, POOL_WIDTH), F32),
                        pltpu.VMEM((HALO, CONV_WIDTH), F32),
                        pltpu.VMEM((HALO, D_FF), F32)],
        compiler_params=pltpu.CompilerParams(
            dimension_semantics=("arbitrary", "arbitrary"), vmem_limit_bytes=VMEM_LIMIT),
        name="mix_ffn",
    )(pool, conv, nsa, x, pbd, pscale, sw, wout, g2, wup, cw, wd)


def _attention_constants():
    c0 = np.arange(N_CMP) * CMP_STRIDE
    s0 = np.arange(N_SEL) * SEL_BLOCK
    ov = (np.minimum(c0[:, None] + CMP_BLOCK, s0[None, :] + SEL_BLOCK)
          - np.maximum(c0[:, None], s0[None, :]))
    frac = np.zeros((N_CHUNK, N_SEL), np.float32)
    frac[:N_CMP] = np.clip(ov, 0, None) / CMP_BLOCK

    def pos_cols(pos, rows):
        x = np.zeros((rows, AUG - HEAD_DIM), np.float32)
        x[:, 0] = pos % POS_SPLIT
        x[:, 1] = pos - pos % POS_SPLIT
        return x

    kpos = np.arange(SEQ)
    kxw = pos_cols(kpos, SEQ)
    kxs = pos_cols(kpos, SEQ)
    kxs[kpos, ROW_SEL - HEAD_DIM + kpos // SEL_BLOCK] = 1.0
    cx = pos_cols(np.arange(N_CHUNK) * CMP_STRIDE + CMP_BLOCK - 1, N_CHUNK)
    return tuple(jnp.asarray(a, BF16) for a in (frac.T, kxs, kxw, cx))


def _block_diag(blocks):
    k = len(blocks)
    rows = [jnp.concatenate([blk if j == i else jnp.zeros_like(blk) for j in range(k)], axis=-1)
            for i, blk in enumerate(blocks)]
    return jnp.concatenate(rows, axis=-2)


def _in_proj_weights(w_in):
    o_q = POOL_WIDTH
    o_kv = o_q + NSA_WIDTH
    o_gate = o_kv + 6 * KV_WIDTH
    o_conv = o_gate + 3 * N_HEADS
    kv = [w_in[..., o_kv + i * KV_WIDTH:o_kv + (i + 1) * KV_WIDTH] for i in range(6)]
    grp = lambda a, g: a[..., g * HEAD_DIM:(g + 1) * HEAD_DIM]
    kvg = [grp(kv[i], g) for g in range(KV_HEADS) for i in (2, 3, 4, 5)]
    kvc = [grp(kv[i], g) for g in range(KV_HEADS) for i in (0, 1)]
    gate = jnp.pad(w_in[..., o_gate:o_conv], ((0, 0), (0, 0), (0, GATE_PAD - 3 * N_HEADS)))
    cols = [w_in[..., :o_kv]] + kvg + kvc + [w_in[..., o_conv:], gate]
    return jnp.concatenate(cols, axis=-1).astype(BF16)


def _compress_weights(pos, w1, w2):
    depth = pos.shape[0]
    pos = jnp.concatenate([pos[:, 0], pos[:, 1]], axis=-1)[:, :, None, :]
    w1 = w1.reshape(depth, 2, CMP_BLOCK, HEAD_DIM, HEAD_DIM)
    w1 = _block_diag([w1[:, 0], w1[:, 1]]).astype(BF16)
    w2 = _block_diag([w2[:, 0], w2[:, 1]]).astype(BF16)
    return pos[:, :CMP_STRIDE], pos[:, CMP_STRIDE:], w1[:, :CMP_STRIDE], w1[:, CMP_STRIDE:], w2


def kernel(x, norm1_g, w_in, pool_w, pool_scale, q_norm_g, k_norm_g, cmp_pos, cmp_w1, cmp_w2,
           sconv_w, w_out, norm2_g, ffn_up, ffn_conv, ffn_down):
    depth = w_in.shape[0]
    selt, kxs, kxw, cx = _attention_constants()
    w = _in_proj_weights(w_in)
    pbd = _block_diag([pool_w[:, i] for i in range(POOL_GROUPS)]).astype(BF16)
    plo, phi, wlo, whi, w2 = _compress_weights(cmp_pos, cmp_w1, cmp_w2)
    qg_t = jnp.broadcast_to(q_norm_g[:, :, None], (depth, HEAD_DIM, TQ))
    kg = jnp.concatenate([k_norm_g[:, 1:3], k_norm_g[:, 0:1]], axis=1)
    g1, g2, pscale = norm1_g[:, None, :], norm2_g[:, None, :], pool_scale[:, None, :]
    wout, wup, wd = w_out.astype(BF16), ffn_up.astype(BF16), ffn_down.astype(BF16)
    for l in range(depth):
        pool, q, kvg, kvc, conv, gate = _in_proj(x, g1, w, l)
        nsa = _nsa(q, kvg, kvc, gate, qg_t, kg, selt, kxs, kxw, cx, plo, phi, wlo, whi, w2, l)
        x = _mix_ffn(pool, conv, nsa, x, pbd, pscale, sconv_w, wout, g2, wup, ffn_conv, wd, l)
    return x
```

```python
import numpy as np
import jax
import jax.numpy as jnp
from jax import lax
from jax.experimental import pallas as pl
from jax.experimental.pallas import tpu as pltpu

D_MODEL = 1024
SEQ = 2048
HEAD_DIM = 64
POOL_WINDOWS = (2, 4, 8, 16)
POOL_GROUPS = 4
POOL_WIDTH = 256
POOL_GDIM = POOL_WIDTH // POOL_GROUPS
NSA_WIDTH = 512
N_HEADS = 8
KV_HEADS = 2
HEADS_PER_KV = N_HEADS // KV_HEADS
KV_WIDTH = KV_HEADS * HEAD_DIM
CONV_WIDTH = 256
CMP_BLOCK = 32
CMP_STRIDE = 16
N_CMP = (SEQ - CMP_BLOCK) // CMP_STRIDE + 1
N_CHUNK = SEQ // CMP_STRIDE
SEL_BLOCK = 64
N_SEL = SEQ // SEL_BLOCK
SEL_TOPN = 8
WINDOW = 512
D_FF = 2816
EPS = 1e-6
NEG_INF = -1e30
FORCE_SCORE = 1e6
QK_SCALE = HEAD_DIM ** -0.5

LANES = 128
GATE_PAD = LANES
KVG_WIDTH = 4 * HEAD_DIM
KVC_WIDTH = 2 * HEAD_DIM
HALO = 16
AUG = 2 * HEAD_DIM
POS_SPLIT = 256
ROW_POS = HEAD_DIM
ROW_SEL = AUG - N_SEL
V_ROWS = HEAD_DIM + 16

TM_IN = 1024
TM_PROJ = 512
TQ = 256
FF_CHUNK = 256
VMEM_TEMP_BYTES = 8 * 1024 * 1024
VMEM_CAP_BYTES = 64 * 1024 * 1024

assert WINDOW % TQ == 0 and SEQ % TQ == 0 and TQ % SEL_BLOCK == 0

F32 = jnp.float32
BF16 = jnp.bfloat16


def _rms(x, g):
    return x * lax.rsqrt(jnp.mean(x * x, axis=-1, keepdims=True) + EPS) * g


def _dot(a, b):
    return jnp.dot(a, b, preferred_element_type=F32)


def _nbytes(shape, dtype):
    return int(np.prod(shape)) * jnp.dtype(dtype).itemsize


def _vmem_limit(pipelined, resident):
    need = (2 * sum(_nbytes(*b) for b in pipelined) + sum(_nbytes(*b) for b in resident)
            + VMEM_TEMP_BYTES)
    assert need <= VMEM_CAP_BYTES, need
    return need


def _layer_spec(a, layer):
    zeros = (0,) * (a.ndim - 1)
    return pl.BlockSpec((None,) + a.shape[1:], lambda *_: (layer,) + zeros,
                        pipeline_mode=pl.Buffered(1))


IN_SEGS = (("pool", POOL_WIDTH), ("q", NSA_WIDTH), ("kvg", KV_HEADS * KVG_WIDTH),
           ("kvc", KV_HEADS * KVC_WIDTH), ("conv", 3 * CONV_WIDTH), ("gate", GATE_PAD))


def _in_proj_kernel(x_ref, g_ref, w_ref, *out_refs):
    h = _rms(x_ref[0], g_ref[...]).astype(BF16)
    off = 0
    for (_, width), o_ref in zip(IN_SEGS, out_refs):
        o_ref[0] = _dot(h, w_ref[:, off:off + width])
        off += width


def _in_proj(x, g, w, layer):
    b, t, d = x.shape
    tm = TM_IN
    out_shape = [jax.ShapeDtypeStruct((b, t, width), F32) for _, width in IN_SEGS]
    out_specs = [pl.BlockSpec((1, tm, width), lambda i, j: (i, j, 0)) for _, width in IN_SEGS]
    tiles = [((tm, d), F32)] + [((tm, width), F32) for _, width in IN_SEGS]
    return pl.pallas_call(
        _in_proj_kernel,
        grid=(b, t // tm),
        in_specs=[pl.BlockSpec((1, tm, d), lambda i, j: (i, j, 0)),
                  _layer_spec(g, layer), _layer_spec(w, layer)],
        out_specs=out_specs,
        out_shape=out_shape,
        compiler_params=pltpu.CompilerParams(
            dimension_semantics=("arbitrary", "arbitrary"),
            vmem_limit_bytes=_vmem_limit(tiles, [(w.shape[1:], w.dtype)])),
        name="in_proj",
    )(x, g, w)


def _score_tile(ka_s, qt_s, jt, s_ref):
    k0 = pl.multiple_of(jt * TQ, TQ)
    ka = ka_s[pl.ds(k0, TQ), :]
    for h in range(HEADS_PER_KV):
        s_ref[h] = _dot(ka, qt_s[h])


def _softmax_tile(s_ref, vt, m_s, acc_s, mode):
    tq = TQ
    if mode != "full":
        key = lax.broadcasted_iota(jnp.int32, (tq, tq), 0)
        qry = lax.broadcasted_iota(jnp.int32, (tq, tq), 1)
        keep = key <= qry if mode == "causal" else key > qry
    for h in range(HEADS_PER_KV):
        s = s_ref[h]
        if mode != "full":
            s = jnp.where(keep, s, NEG_INF)
        m_old = m_s[h]
        m_new = jnp.maximum(m_old, jnp.max(s, axis=0, keepdims=True))
        alpha = jnp.exp(m_old - m_new)
        p = jnp.exp(s - m_new).astype(BF16)
        acc_s[h] = alpha * acc_s[h] + _dot(vt, p)
        m_s[h] = m_new


def _attn_init(m_s, acc_s):
    m_s[...] = jnp.full(m_s.shape, NEG_INF, F32)
    acc_s[...] = jnp.zeros(acc_s.shape, F32)


def _attn_keep_if(live, m_s, acc_s):
    m_s[...] = jnp.where(live, m_s[...], NEG_INF)
    acc_s[...] = jnp.where(live, acc_s[...], 0.0)


def _compress_blocks(kvc_ref, plo_ref, phi_ref, wlo_ref, whi_ref, w2_ref, kg_ref, cx_ref,
                     kca_s, vct_s):
    lo = jnp.zeros((N_CHUNK, 2 * HEAD_DIM), F32)
    hi = jnp.zeros((N_CHUNK, 2 * HEAD_DIM), F32)
    for r in range(CMP_STRIDE):
        x = kvc_ref[0, pl.ds(r, N_CHUNK, stride=CMP_STRIDE), :]
        lo = lo + _dot((x + plo_ref[r]).astype(BF16), wlo_ref[r])
        hi = hi + _dot((x + phi_ref[r]).astype(BF16), whi_ref[r])
    hid = jax.nn.gelu(lo + pltpu.roll(hi, N_CHUNK - 1, 0))
    out = _dot(hid.astype(BF16), w2_ref[...])
    kn = _rms(out[:, 0:HEAD_DIM], kg_ref[2:3, :]).astype(BF16)
    kca_s[...] = jnp.concatenate([kn, cx_ref[...]], axis=1)
    vct_s[...] = out.T[HEAD_DIM:].astype(BF16)


def _nsa_kernel(q_ref, kv_ref, kvc_ref, gate_ref, qg_ref, kg_ref, selt_ref, kxs_ref, kxw_ref,
                cx_ref, plo_ref, phi_ref, wlo_ref, whi_ref, w2_ref, o_ref,
                ksa_s, kwa_s, vst_s, vwt_s, kca_s, vct_s, qt_s, ot2_s, m_s, acc_s, wbuf, sbuf):
    tq = TQ
    g = pl.program_id(1)
    state = (m_s, acc_s)

    def prep_keys():
        ones_row = jnp.where(lax.broadcasted_iota(jnp.int32, (V_ROWS - HEAD_DIM, tq), 0) == 0,
                             1.0, 0.0).astype(BF16)
        for off, gi, ka_s, vt_s, kx_ref in ((0, 0, ksa_s, vst_s, kxs_ref),
                                            (2 * HEAD_DIM, 1, kwa_s, vwt_s, kxw_ref)):
            kn = _rms(kv_ref[0, :, off:off + HEAD_DIM], kg_ref[gi:gi + 1, :]).astype(BF16)
            ka_s[...] = jnp.concatenate([kn, kx_ref[...]], axis=1)
            for j in range(SEQ // tq):
                kv_t = kv_ref[0, j * tq:(j + 1) * tq, off:off + 2 * HEAD_DIM].T
                vt_s[j, 0:HEAD_DIM, :] = kv_t[HEAD_DIM:].astype(BF16)
                vt_s[j, HEAD_DIM:V_ROWS, :] = ones_row
        _compress_blocks(kvc_ref, plo_ref, phi_ref, wlo_ref, whi_ref, w2_ref, kg_ref, cx_ref,
                         kca_s, vct_s)

    prep_keys()

    def query_tile(qi, ot_s):
        t0 = pl.multiple_of(qi * tq, tq)
        rows_q = pl.ds(t0, tq)

        def by_group(rows, r):
            return jnp.where(g == 0, rows[r:r + 1], rows[r + HEADS_PER_KV:r + HEADS_PER_KV + 1])

        gates_t = jax.nn.sigmoid(gate_ref[0, rows_q, :]).T
        q_t = q_ref[0, rows_q, :].T
        row32 = lax.broadcasted_iota(jnp.int32, (ROW_SEL - ROW_POS, tq), 0)
        for h in range(HEADS_PER_KV):
            qh = q_t[h * HEAD_DIM:(h + 1) * HEAD_DIM]
            qn = qh * lax.rsqrt(jnp.mean(qh * qh, axis=0, keepdims=True) + EPS) * qg_ref[...]
            slope = jnp.where(g == 0, 2.0 ** -(h + 1), 2.0 ** -(h + 1 + HEADS_PER_KV)).astype(F32)
            qt_s[h, 0:ROW_POS, :] = (qn * QK_SCALE).astype(BF16)
            qt_s[h, ROW_POS:ROW_SEL, :] = jnp.where(row32 < 2, slope, 0.0).astype(BF16)
            qt_s[h, ROW_SEL:AUG, :] = jnp.zeros((N_SEL, tq), BF16)

        n_back = WINDOW // tq
        win_tiles = [jnp.maximum(qi - back, 0) for back in range(n_back, 0, -1)] + [qi]

        blk_end = lax.broadcasted_iota(jnp.int32, (N_CHUNK, tq), 0) * CMP_STRIDE + (CMP_BLOCK - 1)
        valid = t0 + lax.broadcasted_iota(jnp.int32, (N_CHUNK, tq), 1) >= blk_end
        any_valid = (t0 + lax.broadcasted_iota(jnp.int32, (1, tq), 1) >= CMP_BLOCK - 1).astype(F32)
        imp_t = jnp.zeros((N_SEL, tq), F32)
        cmp_scores = [_dot(kca_s[...], qt_s[h]) for h in range(HEADS_PER_KV)]
        _score_tile(kwa_s, qt_s, win_tiles[0], wbuf.at[0])
        for h in range(HEADS_PER_KV):
            s = jnp.where(valid, cmp_scores[h], NEG_INF)
            e = jnp.exp(s - jnp.max(s, axis=0, keepdims=True))
            p = (e / jnp.sum(e, axis=0, keepdims=True) * any_valid).astype(BF16)
            ot_s[h * HEAD_DIM:(h + 1) * HEAD_DIM, :] = by_group(gates_t, h) * _dot(vct_s[...], p)
            imp_t = imp_t + _dot(selt_ref[...], p)

        for i in range(1, n_back):
            _score_tile(kwa_s, qt_s, win_tiles[i], wbuf.at[i])

        blk = lax.broadcasted_iota(jnp.int32, (N_SEL, tq), 0)
        cur = (t0 + lax.broadcasted_iota(jnp.int32, (N_SEL, tq), 1)) // SEL_BLOCK
        forced = (blk == 0) | (blk == cur) | (blk == cur - 1)
        val = jnp.where(forced, FORCE_SCORE, jnp.where(blk <= cur, imp_t, -FORCE_SCORE))
        rank = jnp.zeros((N_SEL, tq), F32)
        for i in range(N_SEL):
            vi = val[i:i + 1, :]
            rank = rank + jnp.where(blk > i, jnp.where(vi >= val, 1.0, 0.0),
                                    jnp.where(vi > val, 1.0, 0.0))
        not_sel = jnp.where(rank < SEL_TOPN, 0.0, NEG_INF).astype(BF16)
        for h in range(HEADS_PER_KV):
            qt_s[h, ROW_SEL:AUG, :] = not_sel

        def finish(gate_row0):
            for h in range(HEADS_PER_KV):
                rows = slice(h * HEAD_DIM, (h + 1) * HEAD_DIM)
                a = acc_s[h]
                out = a[0:HEAD_DIM] / a[HEAD_DIM:HEAD_DIM + 1]
                ot_s[rows, :] += by_group(gates_t, gate_row0 + h) * out

        _score_tile(ksa_s, qt_s, 0, sbuf.at[0])

        _attn_init(*state)
        for i, back in enumerate(range(n_back, 0, -1)):
            mode = "band" if back == n_back else "full"
            _softmax_tile(wbuf.at[i], vwt_s[win_tiles[i]], *state, mode)
            _attn_keep_if(qi >= back, *state)
            if i == 0:
                _score_tile(kwa_s, qt_s, qi, wbuf.at[n_back])
        _softmax_tile(wbuf.at[n_back], vwt_s[qi], *state, "causal")
        finish(2 * N_HEADS)

        _attn_init(*state)

        def slc_pair(i, carry):
            jt = 2 * i
            _score_tile(ksa_s, qt_s, jt + 1, sbuf.at[1])
            _softmax_tile(sbuf.at[0], vst_s[jt], *state, "full")
            _score_tile(ksa_s, qt_s, jt + 2, sbuf.at[0])
            _softmax_tile(sbuf.at[1], vst_s[jt + 1], *state, "full")
            return carry

        lax.fori_loop(0, qi // 2, slc_pair, 0)

        @pl.when(qi % 2 == 0)
        def _even_tail():
            _softmax_tile(sbuf.at[0], vst_s[qi], *state, "causal")

        @pl.when(qi % 2 == 1)
        def _odd_tail():
            _score_tile(ksa_s, qt_s, qi, sbuf.at[1])
            _softmax_tile(sbuf.at[0], vst_s[qi - 1], *state, "full")
            _softmax_tile(sbuf.at[1], vst_s[qi], *state, "causal")

        finish(N_HEADS)

        o_ref[0, rows_q, :] = ot_s[...].T.astype(BF16)

    def query_tile_pair(i, carry):
        query_tile(2 * i, ot2_s.at[0])
        query_tile(2 * i + 1, ot2_s.at[1])
        return carry

    lax.fori_loop(0, SEQ // tq // 2, query_tile_pair, 0)


def _nsa(q, kvg, kvc, gate, qg_t, kg, selt, kxs, kxw, cx, plo, phi, wlo, whi, w2, layer):
    b, t, _ = q.shape
    tq = TQ
    gq = HEADS_PER_KV * HEAD_DIM
    full = lambda a: pl.BlockSpec(a.shape, lambda i, j: (0,) * a.ndim)
    per_layer = lambda a: _layer_spec(a, layer)
    slabs = [((t, gq), F32), ((t, KVG_WIDTH), F32), ((t, KVC_WIDTH), F32), ((t, GATE_PAD), F32),
             ((t, gq), BF16)]
    scratch = [((t, AUG), BF16), ((t, AUG), BF16),
               ((t // tq, V_ROWS, tq), BF16), ((t // tq, V_ROWS, tq), BF16),
               ((N_CHUNK, AUG), BF16), ((HEAD_DIM, N_CHUNK), BF16),
               ((HEADS_PER_KV, AUG, tq), BF16),
               ((2, gq, tq), F32),
               ((HEADS_PER_KV, 1, tq), F32), ((HEADS_PER_KV, V_ROWS, tq), F32),
               ((WINDOW // tq + 1, HEADS_PER_KV, tq, tq), F32),
               ((2, HEADS_PER_KV, tq, tq), F32)]
    consts = [(a.shape, a.dtype) for a in (selt, kxs, kxw, cx)]
    consts += [(a.shape[1:], a.dtype) for a in (qg_t, kg, plo, phi, wlo, whi, w2)]
    return pl.pallas_call(
        _nsa_kernel,
        grid=(b, KV_HEADS),
        in_specs=[pl.BlockSpec((1, t, gq), lambda i, j: (i, 0, j)),
                  pl.BlockSpec((1, t, KVG_WIDTH), lambda i, j: (i, 0, j)),
                  pl.BlockSpec((1, t, KVC_WIDTH), lambda i, j: (i, 0, j)),
                  pl.BlockSpec((1, t, GATE_PAD), lambda i, j: (i, 0, 0)),
                  per_layer(qg_t), per_layer(kg), full(selt), full(kxs), full(kxw), full(cx),
                  per_layer(plo), per_layer(phi), per_layer(wlo), per_layer(whi),
                  per_layer(w2)],
        out_specs=pl.BlockSpec((1, t, gq), lambda i, j: (i, 0, j)),
        out_shape=jax.ShapeDtypeStruct((b, t, NSA_WIDTH), BF16),
        scratch_shapes=[pltpu.VMEM(shape, dtype) for shape, dtype in scratch],
        compiler_params=pltpu.CompilerParams(
            dimension_semantics=("arbitrary", "arbitrary"),
            vmem_limit_bytes=_vmem_limit(slabs, scratch + consts)),
        name="nsa",
    )(q, kvg, kvc, gate, qg_t, kg, selt, kxs, kxw, cx, plo, phi, wlo, whi, w2)


def _mix_ffn_kernel(pool_ref, conv_ref, nsa_ref, x_ref, pbd_ref, pscale_ref, sw_ref, wout_ref,
                    g2_ref, wup_ref, cw_ref, wd_ref, o_ref,
                    cat_s, h2_s, z_s, ptail_s, ctail_s, atail_s):
    tm = TM_PROJ
    ti = pl.program_id(1)

    @pl.when(ti == 0)
    def _reset():
        ptail_s[...] = jnp.zeros_like(ptail_s)
        ctail_s[...] = jnp.zeros_like(ctail_s)
        atail_s[...] = jnp.zeros_like(atail_s)

    u = pool_ref[0]
    ext = jnp.concatenate([ptail_s[...], u], axis=0)
    sums = []
    acc = ext
    for shift in (1, 2, 4, 8):
        acc = acc + pltpu.roll(acc, shift, 0)
        sums.append(acc[HALO:])
    grp = lax.broadcasted_iota(jnp.int32, (1, POOL_WIDTH), 1) // POOL_GDIM
    wsum = jnp.where(grp == 0, sums[0], jnp.where(grp == 1, sums[1],
                                                  jnp.where(grp == 2, sums[2], sums[3])))
    width = jnp.where(grp == 0, POOL_WINDOWS[0], jnp.where(
        grp == 1, POOL_WINDOWS[1], jnp.where(grp == 2, POOL_WINDOWS[2], POOL_WINDOWS[3])))
    tpos = ti * tm + lax.broadcasted_iota(jnp.int32, (tm, 1), 0) + 1
    cnt = jnp.minimum(tpos, width).astype(F32)
    diff = (wsum / cnt - u).astype(BF16)
    cat_s[:, 0:POOL_WIDTH] = (_dot(diff, pbd_ref[...]) * pscale_ref[...]).astype(BF16)
    ptail_s[...] = u[tm - HALO:]

    cat_s[:, POOL_WIDTH:POOL_WIDTH + NSA_WIDTH] = nsa_ref[0]

    cb = conv_ref[0, :, 0:CONV_WIDTH]
    prod = conv_ref[0, :, CONV_WIDTH:2 * CONV_WIDTH] * conv_ref[0, :, 2 * CONV_WIDTH:3 * CONV_WIDTH]
    pext = jnp.concatenate([ctail_s[...], prod], axis=0)
    conv = (sw_ref[2:3, :] * pext + sw_ref[1:2, :] * pltpu.roll(pext, 1, 0)
            + sw_ref[0:1, :] * pltpu.roll(pext, 2, 0))[HALO:]
    cat_s[:, POOL_WIDTH + NSA_WIDTH:] = (cb * conv).astype(BF16)
    ctail_s[...] = prod[tm - HALO:]

    x1 = x_ref[0] + _dot(cat_s[...], wout_ref[...])
    o_ref[0] = x1
    h2_s[...] = _rms(x1, g2_ref[...]).astype(BF16)

    for c in range(D_FF // FF_CHUNK):
        lo, hi = c * FF_CHUNK, (c + 1) * FF_CHUNK
        a = _dot(h2_s[...], wup_ref[:, lo:hi])
        gate = _dot(h2_s[...], wup_ref[:, D_FF + lo:D_FF + hi])
        aext = jnp.concatenate([atail_s[:, lo:hi], a], axis=0)
        conv = (cw_ref[2:3, lo:hi] * aext + cw_ref[1:2, lo:hi] * pltpu.roll(aext, 1, 0)
                + cw_ref[0:1, lo:hi] * pltpu.roll(aext, 2, 0))[HALO:]
        z_s[:, lo:hi] = (jax.nn.silu(conv) * gate).astype(BF16)
        atail_s[:, lo:hi] = a[tm - HALO:]
    o_ref[0] += _dot(z_s[...], wd_ref[...])


def _mix_ffn(pool, conv, nsa, x, pbd, pscale, sw, wout, g2, wup, cw, wd, layer):
    b, t, d = x.shape
    tm = TM_PROJ
    tile = lambda width: pl.BlockSpec((1, tm, width), lambda i, j: (i, j, 0))
    per_layer = lambda a: _layer_spec(a, layer)
    tiles = [((tm, POOL_WIDTH), F32), ((tm, 3 * CONV_WIDTH), F32), ((tm, NSA_WIDTH), BF16),
             ((tm, d), F32), ((tm, d), F32)]
    scratch = [((tm, d), BF16), ((tm, d), BF16), ((tm, D_FF), BF16),
               ((HALO, POOL_WIDTH), F32), ((HALO, CONV_WIDTH), F32), ((HALO, D_FF), F32)]
    weights = [(a.shape[1:], a.dtype) for a in (pbd, pscale, sw, wout, g2, wup, cw, wd)]
    return pl.pallas_call(
        _mix_ffn_kernel,
        grid=(b, t // tm),
        in_specs=[tile(POOL_WIDTH), tile(3 * CONV_WIDTH), tile(NSA_WIDTH), tile(d),
                  per_layer(pbd), per_layer(pscale), per_layer(sw), per_layer(wout),
                  per_layer(g2), per_layer(wup), per_layer(cw), per_layer(wd)],
        out_specs=tile(d),
        out_shape=jax.ShapeDtypeStruct((b, t, d), F32),
        scratch_shapes=[pltpu.VMEM(shape, dtype) for shape, dtype in scratch],
        compiler_params=pltpu.CompilerParams(
            dimension_semantics=("arbitrary", "arbitrary"),
            vmem_limit_bytes=_vmem_limit(tiles, scratch + weights)),
        name="mix_ffn",
    )(pool, conv, nsa, x, pbd, pscale, sw, wout, g2, wup, cw, wd)


def _attention_constants():
    c0 = np.arange(N_CMP) * CMP_STRIDE
    s0 = np.arange(N_SEL) * SEL_BLOCK
    ov = (np.minimum(c0[:, None] + CMP_BLOCK, s0[None, :] + SEL_BLOCK)
          - np.maximum(c0[:, None], s0[None, :]))
    frac = np.zeros((N_CHUNK, N_SEL), np.float32)
    frac[:N_CMP] = np.clip(ov, 0, None) / CMP_BLOCK

    def pos_cols(pos, rows):
        x = np.zeros((rows, AUG - HEAD_DIM), np.float32)
        x[:, 0] = pos % POS_SPLIT
        x[:, 1] = pos - pos % POS_SPLIT
        return x

    kpos = np.arange(SEQ)
    kxw = pos_cols(kpos, SEQ)
    kxs = pos_cols(kpos, SEQ)
    kxs[kpos, ROW_SEL - HEAD_DIM + kpos // SEL_BLOCK] = 1.0
    cx = pos_cols(np.arange(N_CHUNK) * CMP_STRIDE + CMP_BLOCK - 1, N_CHUNK)
    return tuple(jnp.asarray(a, BF16) for a in (frac.T, kxs, kxw, cx))


def _block_diag(blocks):
    k = len(blocks)
    rows = [jnp.concatenate([blk if j == i else jnp.zeros_like(blk) for j in range(k)], axis=-1)
            for i, blk in enumerate(blocks)]
    return jnp.concatenate(rows, axis=-2)


def _in_proj_weights(w_in):
    o_q = POOL_WIDTH
    o_kv = o_q + NSA_WIDTH
    o_gate = o_kv + 6 * KV_WIDTH
    o_conv = o_gate + 3 * N_HEADS
    kv = [w_in[..., o_kv + i * KV_WIDTH:o_kv + (i + 1) * KV_WIDTH] for i in range(6)]
    grp = lambda a, g: a[..., g * HEAD_DIM:(g + 1) * HEAD_DIM]
    kvg = [grp(kv[i], g) for g in range(KV_HEADS) for i in (2, 3, 4, 5)]
    kvc = [grp(kv[i], g) for g in range(KV_HEADS) for i in (0, 1)]
    gate = jnp.pad(w_in[..., o_gate:o_conv], ((0, 0), (0, 0), (0, GATE_PAD - 3 * N_HEADS)))
    cols = [w_in[..., :o_kv]] + kvg + kvc + [w_in[..., o_conv:], gate]
    return jnp.concatenate(cols, axis=-1).astype(BF16)


def _compress_weights(pos, w1, w2):
    depth = pos.shape[0]
    pos = jnp.concatenate([pos[:, 0], pos[:, 1]], axis=-1)[:, :, None, :]
    w1 = w1.reshape(depth, 2, CMP_BLOCK, HEAD_DIM, HEAD_DIM)
    w1 = _block_diag([w1[:, 0], w1[:, 1]]).astype(BF16)
    w2 = _block_diag([w2[:, 0], w2[:, 1]]).astype(BF16)
    return pos[:, :CMP_STRIDE], pos[:, CMP_STRIDE:], w1[:, :CMP_STRIDE], w1[:, CMP_STRIDE:], w2


def kernel(x, norm1_g, w_in, pool_w, pool_scale, q_norm_g, k_norm_g, cmp_pos, cmp_w1, cmp_w2,
           sconv_w, w_out, norm2_g, ffn_up, ffn_conv, ffn_down):
    depth = w_in.shape[0]
    selt, kxs, kxw, cx = _attention_constants()
    w = _in_proj_weights(w_in)
    pbd = _block_diag([pool_w[:, i] for i in range(POOL_GROUPS)]).astype(BF16)
    plo, phi, wlo, whi, w2 = _compress_weights(cmp_pos, cmp_w1, cmp_w2)
    qg_t = jnp.broadcast_to(q_norm_g[:, :, None], (depth, HEAD_DIM, TQ))
    kg = jnp.concatenate([k_norm_g[:, 1:3], k_norm_g[:, 0:1]], axis=1)
    g1, g2, pscale = norm1_g[:, None, :], norm2_g[:, None, :], pool_scale[:, None, :]
    wout, wup, wd = w_out.astype(BF16), ffn_up.astype(BF16), ffn_down.astype(BF16)
    for l in range(depth):
        pool, q, kvg, kvc, conv, gate = _in_proj(x, g1, w, l)
        nsa = _nsa(q, kvg, kvc, gate, qg_t, kg, selt, kxs, kxw, cx, plo, phi, wlo, whi, w2, l)
        x = _mix_ffn(pool, conv, nsa, x, pbd, pscale, sconv_w, wout, g2, wup, ffn_conv, wd, l)
    return x
```

```python
import numpy as np
import jax
import jax.numpy as jnp
from jax import lax
from jax.experimental import pallas as pl
from jax.experimental.pallas import tpu as pltpu

D_MODEL = 1024
SEQ = 2048
HEAD_DIM = 64
POOL_WINDOWS = (2, 4, 8, 16)
POOL_GROUPS = 4
POOL_WIDTH = 256
POOL_GDIM = POOL_WIDTH // POOL_GROUPS
NSA_WIDTH = 512
N_HEADS = 8
KV_HEADS = 2
HEADS_PER_KV = N_HEADS // KV_HEADS
KV_WIDTH = KV_HEADS * HEAD_DIM
CONV_WIDTH = 256
CMP_BLOCK = 32
CMP_STRIDE = 16
N_CMP = (SEQ - CMP_BLOCK) // CMP_STRIDE + 1
N_CHUNK = SEQ // CMP_STRIDE
SEL_BLOCK = 64
N_SEL = SEQ // SEL_BLOCK
SEL_TOPN = 8
WINDOW = 512
D_FF = 2816
EPS = 1e-6
NEG_INF = -1e30
FORCE_SCORE = 1e6
QK_SCALE = HEAD_DIM ** -0.5

LANES = 128
GATE_PAD = LANES
KVG_WIDTH = 4 * HEAD_DIM
KVC_WIDTH = 2 * HEAD_DIM
HALO = 16
AUG = 2 * HEAD_DIM
POS_SPLIT = 256
ROW_POS = HEAD_DIM
ROW_SEL = AUG - N_SEL
V_ROWS = HEAD_DIM + 16

TM_IN = 1024
TM_PROJ = 512
TQ = 256
FF_CHUNK = 256
VMEM_TEMP_BYTES = 8 * 1024 * 1024
VMEM_CAP_BYTES = 64 * 1024 * 1024

assert WINDOW % TQ == 0 and SEQ % TQ == 0 and TQ % SEL_BLOCK == 0

F32 = jnp.float32
BF16 = jnp.bfloat16


def _rms(x, g):
    return x * lax.rsqrt(jnp.mean(x * x, axis=-1, keepdims=True) + EPS) * g


def _dot(a, b):
    return jnp.dot(a, b, preferred_element_type=F32)


def _nbytes(shape, dtype):
    return int(np.prod(shape)) * jnp.dtype(dtype).itemsize


def _vmem_limit(pipelined, resident):
    need = (2 * sum(_nbytes(*b) for b in pipelined) + sum(_nbytes(*b) for b in resident)
            + VMEM_TEMP_BYTES)
    assert need <= VMEM_CAP_BYTES, need
    return need


def _layer_spec(a, layer):
    zeros = (0,) * (a.ndim - 1)
    return pl.BlockSpec((None,) + a.shape[1:], lambda *_: (layer,) + zeros,
                        pipeline_mode=pl.Buffered(1))


IN_SEGS = (("pool", POOL_WIDTH), ("q", NSA_WIDTH), ("kvg", KV_HEADS * KVG_WIDTH),
           ("kvc", KV_HEADS * KVC_WIDTH), ("conv", 3 * CONV_WIDTH), ("gate", GATE_PAD))


def _in_proj_kernel(x_ref, g_ref, w_ref, *out_refs):
    h = _rms(x_ref[0], g_ref[...]).astype(BF16)
    off = 0
    for (_, width), o_ref in zip(IN_SEGS, out_refs):
        o_ref[0] = _dot(h, w_ref[:, off:off + width])
        off += width


def _in_proj(x, g, w, layer):
    b, t, d = x.shape
    tm = TM_IN
    out_shape = [jax.ShapeDtypeStruct((b, t, width), F32) for _, width in IN_SEGS]
    out_specs = [pl.BlockSpec((1, tm, width), lambda i, j: (i, j, 0)) for _, width in IN_SEGS]
    tiles = [((tm, d), F32)] + [((tm, width), F32) for _, width in IN_SEGS]
    return pl.pallas_call(
        _in_proj_kernel,
        grid=(b, t // tm),
        in_specs=[pl.BlockSpec((1, tm, d), lambda i, j: (i, j, 0)),
                  _layer_spec(g, layer), _layer_spec(w, layer)],
        out_specs=out_specs,
        out_shape=out_shape,
        compiler_params=pltpu.CompilerParams(
            dimension_semantics=("arbitrary", "arbitrary"),
            vmem_limit_bytes=_vmem_limit(tiles, [(w.shape[1:], w.dtype)])),
        name="in_proj",
    )(x, g, w)


def _score_tile(ka_s, qt_s, jt, s_ref):
    k0 = pl.multiple_of(jt * TQ, TQ)
    ka = ka_s[pl.ds(k0, TQ), :]
    for h in range(HEADS_PER_KV):
        s_ref[h] = _dot(ka, qt_s[h])


def _softmax_tile(s_ref, vt, m_s, acc_s, mode):
    tq = TQ
    if mode != "full":
        key = lax.broadcasted_iota(jnp.int32, (tq, tq), 0)
        qry = lax.broadcasted_iota(jnp.int32, (tq, tq), 1)
        keep = key <= qry if mode == "causal" else key > qry
    for h in range(HEADS_PER_KV):
        s = s_ref[h]
        if mode != "full":
            s = jnp.where(keep, s, NEG_INF)
        m_old = m_s[h]
        m_new = jnp.maximum(m_old, jnp.max(s, axis=0, keepdims=True))
        alpha = jnp.exp(m_old - m_new)
        p = jnp.exp(s - m_new).astype(BF16)
        acc_s[h] = alpha * acc_s[h] + _dot(vt, p)
        m_s[h] = m_new


def _attn_init(m_s, acc_s):
    m_s[...] = jnp.full(m_s.shape, NEG_INF, F32)
    acc_s[...] = jnp.zeros(acc_s.shape, F32)


def _attn_keep_if(live, m_s, acc_s):
    m_s[...] = jnp.where(live, m_s[...], NEG_INF)
    acc_s[...] = jnp.where(live, acc_s[...], 0.0)


def _compress_blocks(kvc_ref, plo_ref, phi_ref, wlo_ref, whi_ref, w2_ref, kg_ref, cx_ref,
                     kca_s, vct_s):
    lo = jnp.zeros((N_CHUNK, 2 * HEAD_DIM), F32)
    hi = jnp.zeros((N_CHUNK, 2 * HEAD_DIM), F32)
    for r in range(CMP_STRIDE):
        x = kvc_ref[0, pl.ds(r, N_CHUNK, stride=CMP_STRIDE), :]
        lo = lo + _dot((x + plo_ref[r]).astype(BF16), wlo_ref[r])
        hi = hi + _dot((x + phi_ref[r]).astype(BF16), whi_ref[r])
    hid = jax.nn.gelu(lo + pltpu.roll(hi, N_CHUNK - 1, 0))
    out = _dot(hid.astype(BF16), w2_ref[...])
    kn = _rms(out[:, 0:HEAD_DIM], kg_ref[2:3, :]).astype(BF16)
    kca_s[...] = jnp.concatenate([kn, cx_ref[...]], axis=1)
    vct_s[...] = out.T[HEAD_DIM:].astype(BF16)


def _nsa_kernel(q_ref, kv_ref, kvc_ref, gate_ref, qg_ref, kg_ref, selt_ref, kxs_ref, kxw_ref,
                cx_ref, plo_ref, phi_ref, wlo_ref, whi_ref, w2_ref, o_ref,
                ksa_s, kwa_s, vst_s, vwt_s, kca_s, vct_s, qt_s, ot2_s, m_s, acc_s, wbuf, sbuf):
    tq = TQ
    g = pl.program_id(1)
    state = (m_s, acc_s)

    def prep_keys():
        ones_row = jnp.where(lax.broadcasted_iota(jnp.int32, (V_ROWS - HEAD_DIM, tq), 0) == 0,
                             1.0, 0.0).astype(BF16)
        for off, gi, ka_s, vt_s, kx_ref in ((0, 0, ksa_s, vst_s, kxs_ref),
                                            (2 * HEAD_DIM, 1, kwa_s, vwt_s, kxw_ref)):
            kn = _rms(kv_ref[0, :, off:off + HEAD_DIM], kg_ref[gi:gi + 1, :]).astype(BF16)
            ka_s[...] = jnp.concatenate([kn, kx_ref[...]], axis=1)
            for j in range(SEQ // tq):
                kv_t = kv_ref[0, j * tq:(j + 1) * tq, off:off + 2 * HEAD_DIM].T
                vt_s[j, 0:HEAD_DIM, :] = kv_t[HEAD_DIM:].astype(BF16)
                vt_s[j, HEAD_DIM:V_ROWS, :] = ones_row
        _compress_blocks(kvc_ref, plo_ref, phi_ref, wlo_ref, whi_ref, w2_ref, kg_ref, cx_ref,
                         kca_s, vct_s)

    prep_keys()

    def query_tile(qi, ot_s):
        t0 = pl.multiple_of(qi * tq, tq)
        rows_q = pl.ds(t0, tq)

        def by_group(rows, r):
            return jnp.where(g == 0, rows[r:r + 1], rows[r + HEADS_PER_KV:r + HEADS_PER_KV + 1])

        gates_t = jax.nn.sigmoid(gate_ref[0, rows_q, :]).T
        q_t = q_ref[0, rows_q, :].T
        row32 = lax.broadcasted_iota(jnp.int32, (ROW_SEL - ROW_POS, tq), 0)
        for h in range(HEADS_PER_KV):
            qh = q_t[h * HEAD_DIM:(h + 1) * HEAD_DIM]
            qn = qh * lax.rsqrt(jnp.mean(qh * qh, axis=0, keepdims=True) + EPS) * qg_ref[...]
            slope = jnp.where(g == 0, 2.0 ** -(h + 1), 2.0 ** -(h + 1 + HEADS_PER_KV)).astype(F32)
            qt_s[h, 0:ROW_POS, :] = (qn * QK_SCALE).astype(BF16)
            qt_s[h, ROW_POS:ROW_SEL, :] = jnp.where(row32 < 2, slope, 0.0).astype(BF16)
            qt_s[h, ROW_SEL:AUG, :] = jnp.zeros((N_SEL, tq), BF16)

        n_back = WINDOW // tq
        win_tiles = [jnp.maximum(qi - back, 0) for back in range(n_back, 0, -1)] + [qi]

        blk_end = lax.broadcasted_iota(jnp.int32, (N_CHUNK, tq), 0) * CMP_STRIDE + (CMP_BLOCK - 1)
        valid = t0 + lax.broadcasted_iota(jnp.int32, (N_CHUNK, tq), 1) >= blk_end
        any_valid = (t0 + lax.broadcasted_iota(jnp.int32, (1, tq), 1) >= CMP_BLOCK - 1).astype(F32)
        imp_t = jnp.zeros((N_SEL, tq), F32)
        cmp_scores = [_dot(kca_s[...], qt_s[h]) for h in range(HEADS_PER_KV)]
        _score_tile(kwa_s, qt_s, win_tiles[0], wbuf.at[0])
        for h in range(HEADS_PER_KV):
            s = jnp.where(valid, cmp_scores[h], NEG_INF)
            e = jnp.exp(s - jnp.max(s, axis=0, keepdims=True))
            p = (e / jnp.sum(e, axis=0, keepdims=True) * any_valid).astype(BF16)
            ot_s[h * HEAD_DIM:(h + 1) * HEAD_DIM, :] = by_group(gates_t, h) * _dot(vct_s[...], p)
            imp_t = imp_t + _dot(selt_ref[...], p)

        for i in range(1, n_back + 1):
            _score_tile(kwa_s, qt_s, win_tiles[i], wbuf.at[i])

        blk = lax.broadcasted_iota(jnp.int32, (N_SEL, tq), 0)
        cur = (t0 + lax.broadcasted_iota(jnp.int32, (N_SEL, tq), 1)) // SEL_BLOCK
        forced = (blk == 0) | (blk == cur) | (blk == cur - 1)
        val = jnp.where(forced, FORCE_SCORE, jnp.where(blk <= cur, imp_t, -FORCE_SCORE))
        rank = jnp.zeros((N_SEL, tq), F32)
        for i in range(N_SEL):
            vi = val[i:i + 1, :]
            rank = rank + jnp.where(blk > i, jnp.where(vi >= val, 1.0, 0.0),
                                    jnp.where(vi > val, 1.0, 0.0))
        not_sel = jnp.where(rank < SEL_TOPN, 0.0, NEG_INF).astype(BF16)
        for h in range(HEADS_PER_KV):
            qt_s[h, ROW_SEL:AUG, :] = not_sel

        def finish(gate_row0):
            for h in range(HEADS_PER_KV):
                rows = slice(h * HEAD_DIM, (h + 1) * HEAD_DIM)
                a = acc_s[h]
                out = a[0:HEAD_DIM] / a[HEAD_DIM:HEAD_DIM + 1]
                ot_s[rows, :] += by_group(gates_t, gate_row0 + h) * out

        _attn_init(*state)
        for i, back in enumerate(range(n_back, 0, -1)):
            mode = "band" if back == n_back else "full"
            _softmax_tile(wbuf.at[i], vwt_s[win_tiles[i]], *state, mode)
            _attn_keep_if(qi >= back, *state)
            if i == 0:
                _score_tile(ksa_s, qt_s, 0, sbuf.at[0])
        _softmax_tile(wbuf.at[n_back], vwt_s[qi], *state, "causal")
        finish(2 * N_HEADS)

        _attn_init(*state)

        def slc_pair(i, carry):
            jt = 2 * i
            _score_tile(ksa_s, qt_s, jt + 1, sbuf.at[1])
            _softmax_tile(sbuf.at[0], vst_s[jt], *state, "full")
            _score_tile(ksa_s, qt_s, jt + 2, sbuf.at[0])
            _softmax_tile(sbuf.at[1], vst_s[jt + 1], *state, "full")
            return carry

        lax.fori_loop(0, qi // 2, slc_pair, 0)

        @pl.when(qi % 2 == 0)
        def _even_tail():
            _softmax_tile(sbuf.at[0], vst_s[qi], *state, "causal")

        @pl.when(qi % 2 == 1)
        def _odd_tail():
            _score_tile(ksa_s, qt_s, qi, sbuf.at[1])
            _softmax_tile(sbuf.at[0], vst_s[qi - 1], *state, "full")
            _softmax_tile(sbuf.at[1], vst_s[qi], *state, "causal")

        finish(N_HEADS)

        o_ref[0, rows_q, :] = ot_s[...].T.astype(BF16)

    def query_tile_pair(i, carry):
        query_tile(2 * i, ot2_s.at[0])
        query_tile(2 * i + 1, ot2_s.at[1])
        return carry

    lax.fori_loop(0, SEQ // tq // 2, query_tile_pair, 0)


def _nsa(q, kvg, kvc, gate, qg_t, kg, selt, kxs, kxw, cx, plo, phi, wlo, whi, w2, layer):
    b, t, _ = q.shape
    tq = TQ
    gq = HEADS_PER_KV * HEAD_DIM
    full = lambda a: pl.BlockSpec(a.shape, lambda i, j: (0,) * a.ndim)
    per_layer = lambda a: _layer_spec(a, layer)
    slabs = [((t, gq), F32), ((t, KVG_WIDTH), F32), ((t, KVC_WIDTH), F32), ((t, GATE_PAD), F32),
             ((t, gq), BF16)]
    scratch = [((t, AUG), BF16), ((t, AUG), BF16),
               ((t // tq, V_ROWS, tq), BF16), ((t // tq, V_ROWS, tq), BF16),
               ((N_CHUNK, AUG), BF16), ((HEAD_DIM, N_CHUNK), BF16),
               ((HEADS_PER_KV, AUG, tq), BF16),
               ((2, gq, tq), F32),
               ((HEADS_PER_KV, 1, tq), F32), ((HEADS_PER_KV, V_ROWS, tq), F32),
               ((WINDOW // tq + 1, HEADS_PER_KV, tq, tq), F32),
               ((2, HEADS_PER_KV, tq, tq), F32)]
    consts = [(a.shape, a.dtype) for a in (selt, kxs, kxw, cx)]
    consts += [(a.shape[1:], a.dtype) for a in (qg_t, kg, plo, phi, wlo, whi, w2)]
    return pl.pallas_call(
        _nsa_kernel,
        grid=(b, KV_HEADS),
        in_specs=[pl.BlockSpec((1, t, gq), lambda i, j: (i, 0, j)),
                  pl.BlockSpec((1, t, KVG_WIDTH), lambda i, j: (i, 0, j)),
                  pl.BlockSpec((1, t, KVC_WIDTH), lambda i, j: (i, 0, j)),
                  pl.BlockSpec((1, t, GATE_PAD), lambda i, j: (i, 0, 0)),
                  per_layer(qg_t), per_layer(kg), full(selt), full(kxs), full(kxw), full(cx),
                  per_layer(plo), per_layer(phi), per_layer(wlo), per_layer(whi),
                  per_layer(w2)],
        out_specs=pl.BlockSpec((1, t, gq), lambda i, j: (i, 0, j)),
        out_shape=jax.ShapeDtypeStruct((b, t, NSA_WIDTH), BF16),
        scratch_shapes=[pltpu.VMEM(shape, dtype) for shape, dtype in scratch],
        compiler_params=pltpu.CompilerParams(
            dimension_semantics=("arbitrary", "arbitrary"),
            vmem_limit_bytes=_vmem_limit(slabs, scratch + consts)),
        name="nsa",
    )(q, kvg, kvc, gate, qg_t, kg, selt, kxs, kxw, cx, plo, phi, wlo, whi, w2)


def _mix_ffn_kernel(pool_ref, conv_ref, nsa_ref, x_ref, pbd_ref, pscale_ref, sw_ref, wout_ref,
                    g2_ref, wup_ref, cw_ref, wd_ref, o_ref,
                    cat_s, h2_s, z_s, ptail_s, ctail_s, atail_s):
    tm = TM_PROJ
    ti = pl.program_id(1)

    @pl.when(ti == 0)
    def _reset():
        ptail_s[...] = jnp.zeros_like(ptail_s)
        ctail_s[...] = jnp.zeros_like(ctail_s)
        atail_s[...] = jnp.zeros_like(atail_s)

    u = pool_ref[0]
    ext = jnp.concatenate([ptail_s[...], u], axis=0)
    sums = []
    acc = ext
    for shift in (1, 2, 4, 8):
        acc = acc + pltpu.roll(acc, shift, 0)
        sums.append(acc[HALO:])
    grp = lax.broadcasted_iota(jnp.int32, (1, POOL_WIDTH), 1) // POOL_GDIM
    wsum = jnp.where(grp == 0, sums[0], jnp.where(grp == 1, sums[1],
                                                  jnp.where(grp == 2, sums[2], sums[3])))
    width = jnp.where(grp == 0, POOL_WINDOWS[0], jnp.where(
        grp == 1, POOL_WINDOWS[1], jnp.where(grp == 2, POOL_WINDOWS[2], POOL_WINDOWS[3])))
    tpos = ti * tm + lax.broadcasted_iota(jnp.int32, (tm, 1), 0) + 1
    cnt = jnp.minimum(tpos, width).astype(F32)
    diff = (wsum / cnt - u).astype(BF16)
    cat_s[:, 0:POOL_WIDTH] = (_dot(diff, pbd_ref[...]) * pscale_ref[...]).astype(BF16)
    ptail_s[...] = u[tm - HALO:]

    cat_s[:, POOL_WIDTH:POOL_WIDTH + NSA_WIDTH] = nsa_ref[0]

    cb = conv_ref[0, :, 0:CONV_WIDTH]
    prod = conv_ref[0, :, CONV_WIDTH:2 * CONV_WIDTH] * conv_ref[0, :, 2 * CONV_WIDTH:3 * CONV_WIDTH]
    pext = jnp.concatenate([ctail_s[...], prod], axis=0)
    conv = (sw_ref[2:3, :] * pext + sw_ref[1:2, :] * pltpu.roll(pext, 1, 0)
            + sw_ref[0:1, :] * pltpu.roll(pext, 2, 0))[HALO:]
    cat_s[:, POOL_WIDTH + NSA_WIDTH:] = (cb * conv).astype(BF16)
    ctail_s[...] = prod[tm - HALO:]

    x1 = x_ref[0] + _dot(cat_s[...], wout_ref[...])
    o_ref[0] = x1
    h2_s[...] = _rms(x1, g2_ref[...]).astype(BF16)

    for c in range(D_FF // FF_CHUNK):
        lo, hi = c * FF_CHUNK, (c + 1) * FF_CHUNK
        a = _dot(h2_s[...], wup_ref[:, lo:hi])
        gate = _dot(h2_s[...], wup_ref[:, D_FF + lo:D_FF + hi])
        aext = jnp.concatenate([atail_s[:, lo:hi], a], axis=0)
        conv = (cw_ref[2:3, lo:hi] * aext + cw_ref[1:2, lo:hi] * pltpu.roll(aext, 1, 0)
                + cw_ref[0:1, lo:hi] * pltpu.roll(aext, 2, 0))[HALO:]
        z_s[:, lo:hi] = (jax.nn.silu(conv) * gate).astype(BF16)
        atail_s[:, lo:hi] = a[tm - HALO:]
    o_ref[0] += _dot(z_s[...], wd_ref[...])


def _mix_ffn(pool, conv, nsa, x, pbd, pscale, sw, wout, g2, wup, cw, wd, layer):
    b, t, d = x.shape
    tm = TM_PROJ
    tile = lambda width: pl.BlockSpec((1, tm, width), lambda i, j: (i, j, 0))
    per_layer = lambda a: _layer_spec(a, layer)
    tiles = [((tm, POOL_WIDTH), F32), ((tm, 3 * CONV_WIDTH), F32), ((tm, NSA_WIDTH), BF16),
             ((tm, d), F32), ((tm, d), F32)]
    scratch = [((tm, d), BF16), ((tm, d), BF16), ((tm, D_FF), BF16),
               ((HALO, POOL_WIDTH), F32), ((HALO, CONV_WIDTH), F32), ((HALO, D_FF), F32)]
    weights = [(a.shape[1:], a.dtype) for a in (pbd, pscale, sw, wout, g2, wup, cw, wd)]
    return pl.pallas_call(
        _mix_ffn_kernel,
        grid=(b, t // tm),
        in_specs=[tile(POOL_WIDTH), tile(3 * CONV_WIDTH), tile(NSA_WIDTH), tile(d),
                  per_layer(pbd), per_layer(pscale), per_layer(sw), per_layer(wout),
                  per_layer(g2), per_layer(wup), per_layer(cw), per_layer(wd)],
        out_specs=tile(d),
        out_shape=jax.ShapeDtypeStruct((b, t, d), F32),
        scratch_shapes=[pltpu.VMEM(shape, dtype) for shape, dtype in scratch],
        compiler_params=pltpu.CompilerParams(
            dimension_semantics=("arbitrary", "arbitrary"),
            vmem_limit_bytes=_vmem_limit(tiles, scratch + weights)),
        name="mix_ffn",
    )(pool, conv, nsa, x, pbd, pscale, sw, wout, g2, wup, cw, wd)


def _attention_constants():
    c0 = np.arange(N_CMP) * CMP_STRIDE
    s0 = np.arange(N_SEL) * SEL_BLOCK
    ov = (np.minimum(c0[:, None] + CMP_BLOCK, s0[None, :] + SEL_BLOCK)
          - np.maximum(c0[:, None], s0[None, :]))
    frac = np.zeros((N_CHUNK, N_SEL), np.float32)
    frac[:N_CMP] = np.clip(ov, 0, None) / CMP_BLOCK

    def pos_cols(pos, rows):
        x = np.zeros((rows, AUG - HEAD_DIM), np.float32)
        x[:, 0] = pos % POS_SPLIT
        x[:, 1] = pos - pos % POS_SPLIT
        return x

    kpos = np.arange(SEQ)
    kxw = pos_cols(kpos, SEQ)
    kxs = pos_cols(kpos, SEQ)
    kxs[kpos, ROW_SEL - HEAD_DIM + kpos // SEL_BLOCK] = 1.0
    cx = pos_cols(np.arange(N_CHUNK) * CMP_STRIDE + CMP_BLOCK - 1, N_CHUNK)
    return tuple(jnp.asarray(a, BF16) for a in (frac.T, kxs, kxw, cx))


def _block_diag(blocks):
    k = len(blocks)
    rows = [jnp.concatenate([blk if j == i else jnp.zeros_like(blk) for j in range(k)], axis=-1)
            for i, blk in enumerate(blocks)]
    return jnp.concatenate(rows, axis=-2)


def _in_proj_weights(w_in):
    o_q = POOL_WIDTH
    o_kv = o_q + NSA_WIDTH
    o_gate = o_kv + 6 * KV_WIDTH
    o_conv = o_gate + 3 * N_HEADS
    kv = [w_in[..., o_kv + i * KV_WIDTH:o_kv + (i + 1) * KV_WIDTH] for i in range(6)]
    grp = lambda a, g: a[..., g * HEAD_DIM:(g + 1) * HEAD_DIM]
    kvg = [grp(kv[i], g) for g in range(KV_HEADS) for i in (2, 3, 4, 5)]
    kvc = [grp(kv[i], g) for g in range(KV_HEADS) for i in (0, 1)]
    gate = jnp.pad(w_in[..., o_gate:o_conv], ((0, 0), (0, 0), (0, GATE_PAD - 3 * N_HEADS)))
    cols = [w_in[..., :o_kv]] + kvg + kvc + [w_in[..., o_conv:], gate]
    return jnp.concatenate(cols, axis=-1).astype(BF16)


def _compress_weights(pos, w1, w2):
    depth = pos.shape[0]
    pos = jnp.concatenate([pos[:, 0], pos[:, 1]], axis=-1)[:, :, None, :]
    w1 = w1.reshape(depth, 2, CMP_BLOCK, HEAD_DIM, HEAD_DIM)
    w1 = _block_diag([w1[:, 0], w1[:, 1]]).astype(BF16)
    w2 = _block_diag([w2[:, 0], w2[:, 1]]).astype(BF16)
    return pos[:, :CMP_STRIDE], pos[:, CMP_STRIDE:], w1[:, :CMP_STRIDE], w1[:, CMP_STRIDE:], w2


def kernel(x, norm1_g, w_in, pool_w, pool_scale, q_norm_g, k_norm_g, cmp_pos, cmp_w1, cmp_w2,
           sconv_w, w_out, norm2_g, ffn_up, ffn_conv, ffn_down):
    depth = w_in.shape[0]
    selt, kxs, kxw, cx = _attention_constants()
    w = _in_proj_weights(w_in)
    pbd = _block_diag([pool_w[:, i] for i in range(POOL_GROUPS)]).astype(BF16)
    plo, phi, wlo, whi, w2 = _compress_weights(cmp_pos, cmp_w1, cmp_w2)
    qg_t = jnp.broadcast_to(q_norm_g[:, :, None], (depth, HEAD_DIM, TQ))
    kg = jnp.concatenate([k_norm_g[:, 1:3], k_norm_g[:, 0:1]], axis=1)
    g1, g2, pscale = norm1_g[:, None, :], norm2_g[:, None, :], pool_scale[:, None, :]
    wout, wup, wd = w_out.astype(BF16), ffn_up.astype(BF16), ffn_down.astype(BF16)
    for l in range(depth):
        pool, q, kvg, kvc, conv, gate = _in_proj(x, g1, w, l)
        nsa = _nsa(q, kvg, kvc, gate, qg_t, kg, selt, kxs, kxw, cx, plo, phi, wlo, whi, w2, l)
        x = _mix_ffn(pool, conv, nsa, x, pbd, pscale, sconv_w, wout, g2, wup, ffn_conv, wd, l)
    return x
```
